```python
import math
import jax, jax.numpy as jnp
from jax import lax
import numpy as np

D_MODEL = 4096
BATCH = 8
SEQ = 2048
DEPTH = 4
DEC_BATCH = 8
DEC_SEQ = 32
PAST_LEN = 2048

CHUNK = 64
Q_BLOCK = 128
C_CONV = 1024
CONV_K = 31
H_B = 16
Q_LORA = 768
KV_LORA = 512
QK_NOPE = 128
QK_ROPE = 64
V_HEAD = 128
ROPE_THETA = 10000.0
MLA_SCALE = (QK_NOPE + QK_ROPE) ** -0.5
H_C = 8
D_C = 64
DIFF_SCALE = D_C ** -0.5
N_BUCKETS = 32
REL_MAX_DIST = 1024
N_EXPERTS = 32
TOP_K = 4
D_EXPERT = 1024
SWIGLU_LIMIT = 7.0
SWIGLU_ALPHA = 1.702
MOE_BLOCK = 256
DN_ALPHA = (2 * DEPTH) ** 0.25
DN_BETA = (8 * DEPTH) ** -0.25
LN_EPS = 1e-5
NEG_INF = -1e30
N_IN = 2 * C_CONV + Q_LORA + KV_LORA + QK_ROPE + 3 * H_C * 2 * D_C + 3 * D_MODEL

kernel_name = 'hybrid_stream_conv_mla_diff_moe_step'


def layernorm(x, g, b):
    xf = x.astype(jnp.float32)
    mu = jnp.mean(xf, -1, keepdims=True)
    xc = xf - mu
    var = jnp.mean(xc * xc, -1, keepdims=True)
    return (xc * lax.rsqrt(var + LN_EPS) * g.astype(jnp.float32) + b.astype(jnp.float32)).astype(x.dtype)


def rmsnorm(x, g):
    xf = x.astype(jnp.float32)
    return (xf * lax.rsqrt(jnp.mean(xf * xf, -1, keepdims=True) + LN_EPS) * g.astype(jnp.float32)).astype(x.dtype)


def rope(x, pos):
    d = x.shape[-1]
    half = d // 2
    inv = 1.0 / (ROPE_THETA ** (jnp.arange(half, dtype=jnp.float32) * 2.0 / d))
    ang = pos.astype(jnp.float32)[:, None] * inv[None, :]
    shape = (1, x.shape[1]) + (1,) * (x.ndim - 3) + (half,)
    cos = jnp.cos(ang).reshape(shape)
    sin = jnp.sin(ang).reshape(shape)
    xf = x.astype(jnp.float32)
    x1, x2 = xf[..., :half], xf[..., half:]
    return jnp.concatenate([x1 * cos - x2 * sin, x2 * cos + x1 * sin], -1).astype(x.dtype)


def rel_bucket(rel):
    nb = N_BUCKETS // 2
    max_exact = nb // 2
    ret = jnp.where(rel > 0, nb, 0)
    n = jnp.abs(rel)
    nf = jnp.maximum(n, 1).astype(jnp.float32)
    large = max_exact + (jnp.log(nf / max_exact) / math.log(REL_MAX_DIST / max_exact) * (nb - max_exact)).astype(jnp.int32)
    large = jnp.minimum(large, nb - 1)
    return (ret + jnp.where(n < max_exact, n, large)).astype(jnp.int32)


def chunk_visible(q_pos, k_pos):
    return (k_pos[None, :] // CHUNK) <= (q_pos[:, None] // CHUNK)


def sweep(fn, q, q_pos):
    B, S = q.shape[0], q.shape[1]
    if S <= Q_BLOCK or S % Q_BLOCK:
        return fn(q, q_pos)
    nb = S // Q_BLOCK
    qb = jnp.moveaxis(q.reshape((B, nb, Q_BLOCK) + q.shape[2:]), 1, 0)
    out = lax.map(lambda t: fn(t[0], t[1]), (qb, q_pos.reshape(nb, Q_BLOCK)))
    out = jnp.moveaxis(out, 0, 1)
    return out.reshape((B, S) + out.shape[3:])


def lambda_init(layer):
    return 0.8 - 0.6 * math.exp(-0.3 * layer)


def token_mixer(x, hist, lam_init, w, rel_bias):
    B, S, _ = x.shape
    u = jnp.einsum('bsd,dn->bsn', x, w['w_in'])
    sizes = [C_CONV, C_CONV, Q_LORA, KV_LORA, QK_ROPE, H_C * 2 * D_C, H_C * 2 * D_C, H_C * 2 * D_C,
             D_MODEL, D_MODEL, D_MODEL]
    points = [int(v) for v in np.cumsum(sizes)[:-1]]
    a_v, a_g, cq, ckv, kpe, dq, dk, dv, g_a, g_b, g_c = jnp.split(u, points, axis=-1)
    if hist is None:
        conv_hist = jnp.zeros((B, CONV_K - 1, C_CONV), x.dtype)
        past_len = 0
    else:
        conv_hist, p_ckv, p_kpe, p_dk, p_dv = hist
        past_len = p_ckv.shape[1]
    Sk = past_len + S
    k_pos = jnp.arange(Sk, dtype=jnp.int32)
    q_pos = k_pos[past_len:]

    glu = a_v * jax.nn.sigmoid(a_g)
    xc = jnp.concatenate([conv_hist, glu], axis=1)
    new_conv = xc[:, xc.shape[1] - (CONV_K - 1):]
    conv = lax.conv_general_dilated(xc, w['w_dw'][:, None, :], (1,), 'VALID',
                                    dimension_numbers=('NWC', 'WIO', 'NWC'),
                                    feature_group_count=C_CONV) + w['b_dw']
    y_a = jnp.einsum('bsc,cd->bsd', jax.nn.silu(layernorm(conv, w['cn_g'], w['cn_b'])), w['w_a_out'])

    q = jnp.einsum('bsr,rhe->bshe', rmsnorm(cq, w['qn_g']), w['w_uq'])
    q = jnp.concatenate([q[..., :QK_NOPE], rope(q[..., QK_NOPE:], q_pos)], -1)
    c_kv = rmsnorm(ckv, w['kvn_g'])
    k_pe = rope(kpe, q_pos)
    if hist is None:
        ckv_all, kpe_all = c_kv, k_pe
    else:
        ckv_all = jnp.concatenate([p_ckv, c_kv], axis=1)
        kpe_all = jnp.concatenate([p_kpe, k_pe], axis=1)
    kv = jnp.einsum('bkr,rhe->bkhe', ckv_all, w['w_ukv'])
    k_nope, v_b = kv[..., :QK_NOPE], kv[..., QK_NOPE:]

    def mla_block(qb, qp):
        s = (jnp.einsum('bqhe,bkhe->bhqk', qb[..., :QK_NOPE], k_nope)
             + jnp.einsum('bqhr,bkr->bhqk', qb[..., QK_NOPE:], kpe_all)).astype(jnp.float32) * MLA_SCALE
        s = jnp.where(chunk_visible(qp, k_pos), s, NEG_INF)
        p = jax.nn.softmax(s, axis=-1).astype(v_b.dtype)
        return jnp.einsum('bhqk,bkhe->bqhe', p, v_b)

    o_b = sweep(mla_block, q, q_pos)
    y_b = jnp.einsum('bsf,fd->bsd', o_b.reshape(B, S, H_B * V_HEAD), w['w_b_out'])

    k_c = dk.reshape(B, S, H_C, 2 * D_C)
    v_c = dv.reshape(B, S, H_C, 2 * D_C)
    if hist is None:
        k_all, v_all = k_c, v_c
    else:
        k_all = jnp.concatenate([p_dk, k_c], axis=1)
        v_all = jnp.concatenate([p_dv, v_c], axis=1)
    k_maps = k_all.reshape(B, Sk, H_C, 2, D_C)
    qc = dq.reshape(B, S, H_C, 2, D_C)
    lam = (jnp.exp(jnp.sum(w['lam_q1'] * w['lam_k1']).astype(jnp.float32))
           - jnp.exp(jnp.sum(w['lam_q2'] * w['lam_k2']).astype(jnp.float32)) + lam_init)

    def diff_block(qb, qp):
        s = jnp.einsum('bqhmd,bkhmd->bmhqk', qb, k_maps).astype(jnp.float32) * DIFF_SCALE
        bias = jnp.transpose(rel_bias[rel_bucket(k_pos[None, :] - qp[:, None])], (2, 0, 1)).astype(jnp.float32)
        s = jnp.where(chunk_visible(qp, k_pos), s + bias, NEG_INF)
        p = jax.nn.softmax(s, axis=-1)
        a = (p[:, 0] - lam * p[:, 1]).astype(v_all.dtype)
        return jnp.einsum('bhqk,bkhe->bqhe', a, v_all)

    o_c = sweep(diff_block, qc, q_pos)
    o_c = rmsnorm(o_c, w['sub_g']) * (1.0 - lam_init)
    y_c = jnp.einsum('bsf,fd->bsd', o_c.reshape(B, S, H_C * 2 * D_C), w['w_c_out'])

    merged = jax.nn.sigmoid(g_a) * y_a + jax.nn.sigmoid(g_b) * y_b + jax.nn.sigmoid(g_c) * y_c
    out = jnp.einsum('bsd,de->bse', merged, w['w_out'])
    return out, (new_conv, c_kv, k_pe, k_c, v_c)


def grouped_experts(h, top_i, gates, w_gu, b_gu, w_dn, b_dn):
    T, D = h.shape
    M = T * TOP_K
    blk = min(MOE_BLOCK, max(8, M // N_EXPERTS))
    n_blocks = (M + N_EXPERTS * (blk - 1) + blk - 1) // blk
    P = n_blocks * blk
    flat_e = top_i.reshape(M)
    flat_g = gates.reshape(M)
    order = jnp.argsort(flat_e)
    e_sorted = flat_e[order]
    counts = jnp.bincount(flat_e, length=N_EXPERTS)
    padded = (counts + blk - 1) // blk * blk
    pad_end = jnp.cumsum(padded)
    pad_start = pad_end - padded
    cnt_start = jnp.cumsum(counts) - counts
    dest = pad_start[e_sorted] + jnp.arange(M, dtype=jnp.int32) - cnt_start[e_sorted]
    row_tok = jnp.zeros((P,), jnp.int32).at[dest].set((order // TOP_K).astype(jnp.int32))
    row_gate = jnp.zeros((P,), jnp.float32).at[dest].set(flat_g[order])
    blk_exp = jnp.minimum(jnp.searchsorted(pad_end, jnp.arange(n_blocks) * blk, side='right'),
                          N_EXPERTS - 1).astype(jnp.int32)

    def run_block(args):
        tok_b, gate_b, e = args
        xb = h[tok_b]
        gu = xb @ w_gu[e] + b_gu[e]
        g, up = gu[:, :D_EXPERT], gu[:, D_EXPERT:]
        g = jnp.minimum(g, SWIGLU_LIMIT)
        up = jnp.clip(up, -SWIGLU_LIMIT, SWIGLU_LIMIT)
        act = g * jax.nn.sigmoid(SWIGLU_ALPHA * g) * (up + 1.0)
        yb = act @ w_dn[e] + b_dn[e]
        return yb * gate_b[:, None].astype(yb.dtype)

    out = lax.map(run_block, (row_tok.reshape(n_blocks, blk), row_gate.reshape(n_blocks, blk), blk_exp))
    return jax.ops.segment_sum(out.reshape(P, D), row_tok, num_segments=T)


def moe_ffn(x, w):
    B, S, D = x.shape
    h = x.reshape(B * S, D)
    logits = (h @ w['w_router'] + w['b_router']).astype(jnp.float32)
    top_v, top_i = lax.top_k(logits, TOP_K)
    gates = jax.nn.softmax(top_v, axis=-1)
    y = grouped_experts(h, top_i, gates, w['w_gu'], w['b_gu'], w['w_dn'], w['b_dn'])
    return y.reshape(B, S, D).astype(x.dtype)


def trunk(x, caches, layered, rel_bias):
    outs = ([], [], [], [], [])
    for l in range(DEPTH):
        w = {name: arr[l] for name, arr in layered.items()}
        hist = None if caches is None else tuple(c[l] for c in caches)
        mix, st = token_mixer(x, hist, lambda_init(l), w, rel_bias)
        x = layernorm(DN_ALPHA * x + mix, w['ln1_g'], w['ln1_b'])
        x = layernorm(DN_ALPHA * x + moe_ffn(x, w), w['ln2_g'], w['ln2_b'])
        for buf, s in zip(outs, st):
            buf.append(s)
    return x, tuple(jnp.stack(b) for b in outs)


def setup_inputs(seed: int = 0) -> dict:
    key = jax.random.key(seed)
    ks = iter(jax.random.split(key, 48))

    def nrm(shape, scale):
        return scale * jax.random.normal(next(ks), shape, jnp.float32)

    def gain(shape):
        return 1.0 + nrm(shape, 0.02)

    L = DEPTH
    return {
        'x_prompt': nrm((BATCH, SEQ, D_MODEL), 1.0),
        'x_sample': nrm((DEC_BATCH, DEC_SEQ, D_MODEL), 1.0),
        'cache_conv': nrm((L, DEC_BATCH, CONV_K - 1, C_CONV), 0.5),
        'cache_mla_ckv': nrm((L, DEC_BATCH, PAST_LEN, KV_LORA), 1.0),
        'cache_mla_kpe': nrm((L, DEC_BATCH, PAST_LEN, QK_ROPE), 1.0),
        'cache_diff_k': nrm((L, DEC_BATCH, PAST_LEN, H_C, 2 * D_C), 1.0),
        'cache_diff_v': nrm((L, DEC_BATCH, PAST_LEN, H_C, 2 * D_C), 1.0),
        'w_in': nrm((L, D_MODEL, N_IN), D_MODEL ** -0.5),
        'w_dw': nrm((L, CONV_K, C_CONV), CONV_K ** -0.5),
        'b_dw': nrm((L, C_CONV), 0.02),
        'cn_g': gain((L, C_CONV)),
        'cn_b': nrm((L, C_CONV), 0.02),
        'w_a_out': nrm((L, C_CONV, D_MODEL), DN_BETA * C_CONV ** -0.5),
        'qn_g': gain((L, Q_LORA)),
        'w_uq': nrm((L, Q_LORA, H_B, QK_NOPE + QK_ROPE), Q_LORA ** -0.5),
        'kvn_g': gain((L, KV_LORA)),
        'w_ukv': nrm((L, KV_LORA, H_B, QK_NOPE + V_HEAD), KV_LORA ** -0.5),
        'w_b_out': nrm((L, H_B * V_HEAD, D_MODEL), DN_BETA * (H_B * V_HEAD) ** -0.5),
        'lam_q1': nrm((L, D_C), 0.1),
        'lam_k1': nrm((L, D_C), 0.1),
        'lam_q2': nrm((L, D_C), 0.1),
        'lam_k2': nrm((L, D_C), 0.1),
        'sub_g': gain((L, 2 * D_C)),
        'w_c_out': nrm((L, H_C * 2 * D_C, D_MODEL), DN_BETA * (H_C * 2 * D_C) ** -0.5),
        'rel_bias': nrm((N_BUCKETS, H_C), 0.5),
        'w_out': nrm((L, D_MODEL, D_MODEL), DN_BETA * D_MODEL ** -0.5),
        'ln1_g': gain((L, D_MODEL)),
        'ln1_b': nrm((L, D_MODEL), 0.02),
        'w_router': nrm((L, D_MODEL, N_EXPERTS), D_MODEL ** -0.5),
        'b_router': nrm((L, N_EXPERTS), 0.01),
        'w_gu': nrm((L, N_EXPERTS, D_MODEL, 2 * D_EXPERT), D_MODEL ** -0.5),
        'b_gu': nrm((L, N_EXPERTS, 2 * D_EXPERT), 0.02),
        'w_dn': nrm((L, N_EXPERTS, D_EXPERT, D_MODEL), DN_BETA * D_EXPERT ** -0.5),
        'b_dn': nrm((L, N_EXPERTS, D_MODEL), 0.02),
        'ln2_g': gain((L, D_MODEL)),
        'ln2_b': nrm((L, D_MODEL), 0.02),
    }


def reference(x_prompt, x_sample, cache_conv, cache_mla_ckv, cache_mla_kpe, cache_diff_k, cache_diff_v,
              w_in, w_dw, b_dw, cn_g, cn_b, w_a_out, qn_g, w_uq, kvn_g, w_ukv, w_b_out,
              lam_q1, lam_k1, lam_q2, lam_k2, sub_g, w_c_out, rel_bias, w_out, ln1_g, ln1_b,
              w_router, b_router, w_gu, b_gu, w_dn, b_dn, ln2_g, ln2_b):
    layered = dict(w_in=w_in, w_dw=w_dw, b_dw=b_dw, cn_g=cn_g, cn_b=cn_b, w_a_out=w_a_out,
                   qn_g=qn_g, w_uq=w_uq, kvn_g=kvn_g, w_ukv=w_ukv, w_b_out=w_b_out,
                   lam_q1=lam_q1, lam_k1=lam_k1, lam_q2=lam_q2, lam_k2=lam_k2, sub_g=sub_g,
                   w_c_out=w_c_out, w_out=w_out, ln1_g=ln1_g, ln1_b=ln1_b,
                   w_router=w_router, b_router=b_router, w_gu=w_gu, b_gu=b_gu,
                   w_dn=w_dn, b_dn=b_dn, ln2_g=ln2_g, ln2_b=ln2_b)
    y_prompt, p_state = trunk(x_prompt, None, layered, rel_bias)
    caches = (cache_conv, cache_mla_ckv, cache_mla_kpe, cache_diff_k, cache_diff_v)
    y_sample, s_state = trunk(x_sample, caches, layered, rel_bias)
    p_conv, p_ckv, p_kpe, p_dk, p_dv = p_state
    s_conv, s_ckv, s_kpe, s_dk, s_dv = s_state
    return (y_prompt, y_sample, p_conv, p_ckv, p_kpe, p_dk, p_dv, s_conv, s_ckv, s_kpe, s_dk, s_dv)
```

```python
import functools
import math

import numpy as np
import jax
import jax.numpy as jnp
from jax import lax
from jax.experimental import pallas as pl
from jax.experimental.pallas import tpu as pltpu

BF16 = jnp.bfloat16
F32 = jnp.float32

CHUNK = 64
ROPE_THETA = 10000.0
N_BUCKETS = 32
REL_MAX_DIST = 1024
TOP_K = 4
SWIGLU_LIMIT = 7.0
SWIGLU_ALPHA = 1.702
LN_EPS = 1e-5
NEG_INF = -1e30

VMEM_LIMIT_BYTES = 56 * 1024 * 1024
LANES = 128
BF16_SUBLANES = 16

MM_TM = 1280
MM_TN = 512
ATT_TQ = 256
ATT_TK = 256
CONV_TS = 256
CONV_ROWS = 32
LN_TM = 256
MOE_BLK = 256
COMBINE_TC = 64


def _pick(n, target, mult):
    best = None
    for d in range(mult, min(n, target) + 1, mult):
        if n % d == 0:
            best = d
    return n if best is None else best


def _params(*sem):
    return pltpu.CompilerParams(dimension_semantics=sem, vmem_limit_bytes=VMEM_LIMIT_BYTES)


def _fused_mm_kernel(*refs, pair_x, n_x, n_w, n_e, epilogue):
    x_refs = refs[:n_x]
    w_refs = refs[n_x:n_x + n_w]
    e_refs = refs[n_x + n_w:n_x + n_w + n_e]
    o_refs = refs[n_x + n_w + n_e:]
    accs = [jnp.dot(x_refs[xi][...], w_refs[k][...], preferred_element_type=F32)
            for k, xi in enumerate(pair_x)]
    outs = epilogue(accs, [e[...] for e in e_refs])
    for o_ref, o in zip(o_refs, outs):
        o_ref[...] = o.astype(o_ref.dtype)


def fused_matmul(xs, pairs, extras, epilogue, outs, *, m_rows, n_tiles, tm, row_off=0, name=None):
    assert m_rows % tm == 0
    grid = (m_rows // tm, n_tiles)
    in_specs = []
    for x in xs:
        in_specs.append(pl.BlockSpec((tm, x.shape[1]), lambda i, j: (i + row_off, 0)))
    for _, w, wn in pairs:
        assert w.shape[1] == n_tiles * wn
        in_specs.append(pl.BlockSpec((w.shape[0], wn), lambda i, j: (0, j)))
    for _, bshape, imap in extras:
        in_specs.append(pl.BlockSpec(bshape, imap))
    out_shape = [jax.ShapeDtypeStruct((m_rows, n), dt) for n, _, dt in outs]
    out_specs = [pl.BlockSpec((tm, bn), lambda i, j: (i, j)) for _, bn, _ in outs]
    kern = functools.partial(_fused_mm_kernel, pair_x=tuple(p[0] for p in pairs), n_x=len(xs),
                             n_w=len(pairs), n_e=len(extras), epilogue=epilogue)
    res = pl.pallas_call(
        kern, grid=grid, in_specs=in_specs, out_specs=out_specs, out_shape=out_shape,
        compiler_params=_params("parallel", "arbitrary"), name=name,
    )(*xs, *[p[1] for p in pairs], *[e[0] for e in extras])
    return res


def _rms(x, g):
    return x * lax.rsqrt(jnp.mean(x * x, axis=-1, keepdims=True) + LN_EPS) * g


def _layernorm(x, g, b):
    mu = jnp.mean(x, axis=-1, keepdims=True)
    xc = x - mu
    var = jnp.mean(xc * xc, axis=-1, keepdims=True)
    return xc * lax.rsqrt(var + LN_EPS) * g + b


def _ln_kernel(t_ref, g_ref, b_ref, o_ref, ob_ref):
    y = _layernorm(t_ref[...], g_ref[...], b_ref[...])
    o_ref[...] = y
    ob_ref[...] = y.astype(BF16)


def layernorm_rows(t, g, b):
    m, d = t.shape
    tm = _pick(m, LN_TM, BF16_SUBLANES)
    return pl.pallas_call(
        _ln_kernel, grid=(m // tm,),
        in_specs=[pl.BlockSpec((tm, d), lambda i: (i, 0)),
                  pl.BlockSpec((1, d), lambda i: (0, 0)),
                  pl.BlockSpec((1, d), lambda i: (0, 0))],
        out_specs=[pl.BlockSpec((tm, d), lambda i: (i, 0)), pl.BlockSpec((tm, d), lambda i: (i, 0))],
        out_shape=[jax.ShapeDtypeStruct((m, d), F32), jax.ShapeDtypeStruct((m, d), BF16)],
        compiler_params=_params("parallel"), name="layernorm_rows",
    )(t, g.reshape(1, d), b.reshape(1, d))


def _conv_kernel(prev_ref, hist_ref, x_ref, w_ref, bdw_ref, g_ref, b_ref, o_ref, win_ref, *, ts, halo, taps, rows):
    @pl.when(pl.program_id(1) == 0)
    def _():
        win_ref[0:halo, :] = hist_ref[0]

    @pl.when(pl.program_id(1) != 0)
    def _():
        win_ref[0:halo, :] = prev_ref[...]

    win_ref[halo:halo + ts, :] = x_ref[...]
    lead = halo - (taps - 1)

    for r0 in range(0, ts, rows):
        acc = jnp.zeros((rows, x_ref.shape[1]), F32) + bdw_ref[...]
        for k in range(taps):
            acc = acc + w_ref[k:k + 1, :] * win_ref[r0 + lead + k:r0 + lead + k + rows, :]
        y = _layernorm(acc, g_ref[...], b_ref[...])
        o_ref[r0:r0 + rows, :] = (y * jax.nn.sigmoid(y)).astype(o_ref.dtype)


def conv_module(glu, row_off, nb, s, hist, w_dw, b_dw, cn_g, cn_b):
    c = glu.shape[1]
    taps = w_dw.shape[0]
    halo = hist.shape[1]
    ts = _pick(s, CONV_TS, halo)
    rows = _pick(ts, CONV_ROWS, 8)
    nblk = s // ts
    assert row_off % ts == 0 and ts % halo == 0
    kern = functools.partial(_conv_kernel, ts=ts, halo=halo, taps=taps, rows=rows)
    off_main = row_off // ts
    off_prev = row_off // halo
    per = ts // halo
    return pl.pallas_call(
        kern, grid=(nb, nblk),
        in_specs=[
            pl.BlockSpec((halo, c), lambda b, i: (off_prev + jnp.maximum((b * nblk + i) * per - 1, 0), 0)),
            pl.BlockSpec((1, halo, c), lambda b, i: (b, 0, 0)),
            pl.BlockSpec((ts, c), lambda b, i: (off_main + b * nblk + i, 0)),
            pl.BlockSpec((taps, c), lambda b, i: (0, 0)),
            pl.BlockSpec((1, c), lambda b, i: (0, 0)),
            pl.BlockSpec((1, c), lambda b, i: (0, 0)),
            pl.BlockSpec((1, c), lambda b, i: (0, 0)),
        ],
        out_specs=pl.BlockSpec((ts, c), lambda b, i: (b * nblk + i, 0)),
        out_shape=jax.ShapeDtypeStruct((nb * s, c), BF16),
        scratch_shapes=[pltpu.VMEM((halo + ts, c), F32)],
        compiler_params=_params("parallel", "arbitrary"), name="conv_module",
    )(glu, hist, glu, w_dw, b_dw.reshape(1, c), cn_g.reshape(1, c), cn_b.reshape(1, c))


def _block_plan(sq, sk, q_off, tq, tk):
    plan = []
    for q0 in range(0, sq, tq):
        k_any = min(sk, ((q_off + q0 + tq - 1) // CHUNK + 1) * CHUNK)
        k_all = min(sk, ((q_off + q0) // CHUNK + 1) * CHUNK)
        blocks = []
        for k0 in range(0, k_any, tk):
            kl = min(tk, k_any - k0)
            blocks.append((k0, kl, k0 + kl > k_all))
        plan.append((q0, blocks))
    return plan


def _online_step(s, v, m, l, acc):
    m_new = jnp.maximum(m, jnp.max(s, axis=-1, keepdims=True))
    a = jnp.exp(m - m_new)
    p = jnp.exp(s - m_new)
    l = a * l + jnp.sum(p, axis=-1, keepdims=True)
    acc = a * acc + jnp.dot(p.astype(v.dtype), v, preferred_element_type=F32)
    return m_new, l, acc


def _qk(q, k):
    return lax.dot_general(q, k, (((1,), (1,)), ((), ())), preferred_element_type=F32)


def _mla_kernel(q_ref, k_ref, v_ref, o_ref, *, plan, tq, q_off):
    dv = v_ref.shape[1]
    for q0, blocks in plan:
        q = q_ref[q0:q0 + tq, :]
        m = jnp.full((tq, 1), -jnp.inf, F32)
        l = jnp.zeros((tq, 1), F32)
        acc = jnp.zeros((tq, dv), F32)
        for k0, kl, masked in blocks:
            s = _qk(q, k_ref[k0:k0 + kl, :])
            if masked:
                qp = q_off + q0 + lax.broadcasted_iota(jnp.int32, (tq, kl), 0)
                kp = k0 + lax.broadcasted_iota(jnp.int32, (tq, kl), 1)
                s = jnp.where(kp // CHUNK <= qp // CHUNK, s, NEG_INF)
            m, l, acc = _online_step(s, v_ref[k0:k0 + kl, :], m, l, acc)
        o_ref[q0:q0 + tq, :] = (acc / l).astype(o_ref.dtype)


def mla_attention(q, k, v, *, nb, nh, sq, sk, q_off, q_row_off):
    dq = q.shape[1] // nh
    dv = v.shape[1] // nh
    tq = _pick(sq, ATT_TQ, BF16_SUBLANES)
    tk = _pick(sk, ATT_TK, BF16_SUBLANES) if sk % ATT_TK else ATT_TK
    plan = _block_plan(sq, sk, q_off, tq, tk)
    assert q_row_off % sq == 0
    qb = q_row_off // sq
    kern = functools.partial(_mla_kernel, plan=plan, tq=tq, q_off=q_off)
    return pl.pallas_call(
        kern, grid=(nb, nh),
        in_specs=[pl.BlockSpec((sq, dq), lambda b, h: (qb + b, h)),
                  pl.BlockSpec((sk, dq), lambda b, h: (b, h)),
                  pl.BlockSpec((sk, dv), lambda b, h: (b, h))],
        out_specs=pl.BlockSpec((sq, dv), lambda b, h: (b, h)),
        out_shape=jax.ShapeDtypeStruct((nb * sq, nh * dv), BF16),
        compiler_params=_params("parallel", "parallel"), name="mla_attention",
    )(q, k, v)


def _diff_kernel(q_ref, k_ref, v_ref, bias_ref, lq1_ref, lk1_ref, lq2_ref, lk2_ref, sg_ref, o_ref, *,
                 plan, tq, tk, lam_init, bias_blocked):
    dh = q_ref.shape[1]
    half = dh // 2
    lam = (jnp.exp(jnp.sum(lq1_ref[...] * lk1_ref[...], axis=-1, keepdims=True))
           - jnp.exp(jnp.sum(lq2_ref[...] * lk2_ref[...], axis=-1, keepdims=True)) + lam_init)
    lane = lax.broadcasted_iota(jnp.int32, (tq, dh), 1)
    for q0, blocks in plan:
        q = q_ref[q0:q0 + tq, :].astype(F32)
        q2 = jnp.concatenate([jnp.where(lane < half, q, 0.0), jnp.where(lane >= half, q, 0.0)],
                             axis=0).astype(q_ref.dtype)
        m = jnp.full((2 * tq, 1), -jnp.inf, F32)
        l = jnp.zeros((2 * tq, 1), F32)
        acc = jnp.zeros((2 * tq, v_ref.shape[1]), F32)
        for k0, kl, _ in blocks:
            if bias_blocked:
                bias = bias_ref[0, (q0 - k0) // tk]
            else:
                bias = bias_ref[0, 0, :, k0:k0 + kl]
            s = _qk(q2, k_ref[k0:k0 + kl, :]) + jnp.concatenate([bias, bias], axis=0)
            m, l, acc = _online_step(s, v_ref[k0:k0 + kl, :], m, l, acc)
        o = acc / l
        o = o[:tq] - lam * o[tq:]
        o = _rms(o, sg_ref[...]) * (1.0 - lam_init)
        o_ref[q0:q0 + tq, :] = o.astype(o_ref.dtype)


def diff_attention(q, k, v, bias, lam_vecs, sub_g, lam_init, *, nb, nh, sq, sk, q_off, q_row_off):
    dh = q.shape[1] // nh
    tq = _pick(sq, ATT_TQ, BF16_SUBLANES)
    tk = _pick(sk, ATT_TK, BF16_SUBLANES) if sk % ATT_TK else ATT_TK
    plan = _block_plan(sq, sk, q_off, tq, tk)
    bias_blocked = bias.shape[2:] == (tq, tk) and sq > tq
    qb = q_row_off // sq
    kern = functools.partial(_diff_kernel, plan=plan, tq=tq, tk=tk, lam_init=lam_init, bias_blocked=bias_blocked)
    vec = lambda b, h: (0, 0)
    return pl.pallas_call(
        kern, grid=(nb, nh),
        in_specs=[pl.BlockSpec((sq, dh), lambda b, h: (qb + b, h)),
                  pl.BlockSpec((sk, dh), lambda b, h: (b, h)),
                  pl.BlockSpec((sk, dh), lambda b, h: (b, h)),
                  pl.BlockSpec((1,) + bias.shape[1:], lambda b, h: (h, 0, 0, 0)),
                  pl.BlockSpec((1, dh // 2), vec), pl.BlockSpec((1, dh // 2), vec),
                  pl.BlockSpec((1, dh // 2), vec), pl.BlockSpec((1, dh // 2), vec),
                  pl.BlockSpec((1, dh), vec)],
        out_specs=pl.BlockSpec((sq, dh), lambda b, h: (b, h)),
        out_shape=jax.ShapeDtypeStruct((nb * sq, nh * dh), BF16),
        compiler_params=_params("parallel", "parallel"), name="diff_attention",
    )(q, k, v, bias, *[x.reshape(1, -1) for x in lam_vecs], sub_g.reshape(1, dh))


def _router_epilogue(accs, extras):
    logits = accs[0] + extras[0]
    lane = lax.broadcasted_iota(jnp.int32, logits.shape, 1).astype(F32)
    vals, idxs = [], []
    for _ in range(TOP_K):
        mx = jnp.max(logits, axis=-1, keepdims=True)
        idx = jnp.min(jnp.where(logits == mx, lane, float(logits.shape[1])), axis=-1, keepdims=True)
        vals.append(mx)
        idxs.append(idx)
        logits = jnp.where(lane == idx, -jnp.inf, logits)
    es = [jnp.exp(v - vals[0]) for v in vals]
    den = es[0]
    for e in es[1:]:
        den = den + e
    idx_out = jnp.zeros(logits.shape, F32)
    gate_out = jnp.zeros(logits.shape, F32)
    for k in range(TOP_K):
        idx_out = jnp.where(lane == float(k), idxs[k], idx_out)
        gate_out = jnp.where(lane == float(k), es[k] / den, gate_out)
    return [idx_out.astype(jnp.int32), gate_out]


def _gather_kernel(tok_ref, src_ref, o_ref, sem):
    n = o_ref.shape[0]

    def start(r, c):
        pltpu.make_async_copy(src_ref.at[pl.ds(tok_ref[0, 0, r], 1), :], o_ref.at[pl.ds(r, 1), :], sem).start()
        return c

    def wait(r, c):
        pltpu.make_async_copy(src_ref.at[pl.ds(0, 1), :], o_ref.at[pl.ds(r, 1), :], sem).wait()
        return c

    lax.fori_loop(0, n, start, 0)
    lax.fori_loop(0, n, wait, 0)


def gather_rows(src, row_idx, blk):
    p = row_idx.shape[0]
    d = src.shape[1]
    nblk = p // blk
    return pl.pallas_call(
        _gather_kernel, grid=(nblk,),
        in_specs=[pl.BlockSpec((1, 1, blk), lambda i: (i, 0, 0), memory_space=pltpu.SMEM),
                  pl.BlockSpec(memory_space=pl.ANY)],
        out_specs=pl.BlockSpec((blk, d), lambda i: (i, 0)),
        out_shape=jax.ShapeDtypeStruct((p, d), src.dtype),
        scratch_shapes=[pltpu.SemaphoreType.DMA(())],
        compiler_params=_params("arbitrary"), name="gather_rows",
    )(row_idx.reshape(nblk, 1, blk), src)


def _expert_up_kernel(be_ref, x_ref, w_ref, b_ref, o_ref):
    de = o_ref.shape[1]
    gu = jnp.dot(x_ref[...].astype(BF16), w_ref[0], preferred_element_type=F32) + b_ref[0]
    g = jnp.minimum(gu[:, :de], SWIGLU_LIMIT)
    up = jnp.clip(gu[:, de:], -SWIGLU_LIMIT, SWIGLU_LIMIT)
    o_ref[...] = (g * jax.nn.sigmoid(SWIGLU_ALPHA * g) * (up + 1.0)).astype(o_ref.dtype)


def _expert_down_kernel(be_ref, a_ref, w_ref, b_ref, gate_ref, o_ref):
    y = jnp.dot(a_ref[...], w_ref[0], preferred_element_type=F32) + b_ref[0]
    o_ref[...] = y * gate_ref[...]


def expert_ffn(xs, blk_exp, row_gate, w_gu, b_gu, w_dn, b_dn, blk):
    p, d = xs.shape
    e, _, n2 = w_gu.shape
    de = n2 // 2
    nblk = p // blk
    act = pl.pallas_call(
        _expert_up_kernel,
        grid_spec=pltpu.PrefetchScalarGridSpec(
            num_scalar_prefetch=1, grid=(nblk,),
            in_specs=[pl.BlockSpec((blk, d), lambda i, be: (i, 0)),
                      pl.BlockSpec((1, d, n2), lambda i, be: (be[i], 0, 0)),
                      pl.BlockSpec((1, 1, n2), lambda i, be: (be[i], 0, 0))],
            out_specs=pl.BlockSpec((blk, de), lambda i, be: (i, 0))),
        out_shape=jax.ShapeDtypeStruct((p, de), BF16),
        compiler_params=_params("arbitrary"), name="expert_up",
    )(blk_exp, xs, w_gu, b_gu.reshape(e, 1, n2))
    return pl.pallas_call(
        _expert_down_kernel,
        grid_spec=pltpu.PrefetchScalarGridSpec(
            num_scalar_prefetch=1, grid=(nblk,),
            in_specs=[pl.BlockSpec((blk, de), lambda i, be: (i, 0)),
                      pl.BlockSpec((1, de, d), lambda i, be: (be[i], 0, 0)),
                      pl.BlockSpec((1, 1, d), lambda i, be: (be[i], 0, 0)),
                      pl.BlockSpec((blk, 1), lambda i, be: (i, 0))],
            out_specs=pl.BlockSpec((blk, d), lambda i, be: (i, 0))),
        out_shape=jax.ShapeDtypeStruct((p, d), F32),
        compiler_params=_params("arbitrary"), name="expert_down",
    )(blk_exp, act, w_dn, b_dn.reshape(e, 1, d), row_gate.reshape(p, 1))


def _combine_kernel(dest_ref, y_ref, x_ref, g_ref, b_ref, o_ref, ob_ref, buf, sem, *, alpha):
    tc = x_ref.shape[0]
    n = TOP_K * tc

    def start(r, c):
        pltpu.make_async_copy(y_ref.at[pl.ds(dest_ref[0, 0, r], 1), :], buf.at[pl.ds(r, 1), :], sem).start()
        return c

    def wait(r, c):
        pltpu.make_async_copy(y_ref.at[pl.ds(0, 1), :], buf.at[pl.ds(r, 1), :], sem).wait()
        return c

    lax.fori_loop(0, n, start, 0)
    lax.fori_loop(0, n, wait, 0)
    y = buf[0:tc, :]
    for k in range(1, TOP_K):
        y = y + buf[k * tc:(k + 1) * tc, :]
    out = _layernorm(alpha * x_ref[...] + y, g_ref[...], b_ref[...])
    o_ref[...] = out
    ob_ref[...] = out.astype(BF16)


def moe_combine_ln(y_rows, dest, x, g, b, alpha):
    t, d = x.shape
    tc = _pick(t, COMBINE_TC, BF16_SUBLANES)
    nblk = t // tc
    dest_blocks = dest.reshape(nblk, tc, TOP_K).transpose(0, 2, 1).reshape(nblk, 1, TOP_K * tc)
    kern = functools.partial(_combine_kernel, alpha=alpha)
    return pl.pallas_call(
        kern, grid=(nblk,),
        in_specs=[pl.BlockSpec((1, 1, TOP_K * tc), lambda i: (i, 0, 0), memory_space=pltpu.SMEM),
                  pl.BlockSpec(memory_space=pl.ANY),
                  pl.BlockSpec((tc, d), lambda i: (i, 0)),
                  pl.BlockSpec((1, d), lambda i: (0, 0)),
                  pl.BlockSpec((1, d), lambda i: (0, 0))],
        out_specs=[pl.BlockSpec((tc, d), lambda i: (i, 0)), pl.BlockSpec((tc, d), lambda i: (i, 0))],
        out_shape=[jax.ShapeDtypeStruct((t, d), F32), jax.ShapeDtypeStruct((t, d), BF16)],
        scratch_shapes=[pltpu.VMEM((TOP_K * tc, d), F32), pltpu.SemaphoreType.DMA(())],
        compiler_params=_params("arbitrary"), name="moe_combine_ln",
    )(dest_blocks, y_rows, x, g.reshape(1, d), b.reshape(1, d))


def _route_plan(top_i, gates, n_exp, blk):
    t = top_i.shape[0]
    m = t * TOP_K
    n_blocks = (m + n_exp * (blk - 1) + blk - 1) // blk
    p = n_blocks * blk
    flat_e = top_i.reshape(m)
    onehot = (flat_e[:, None] == jnp.arange(n_exp, dtype=jnp.int32)[None, :]).astype(jnp.int32)
    csum = jnp.cumsum(onehot, axis=0)
    counts = csum[-1]
    rank = jnp.take_along_axis(csum, flat_e[:, None], axis=1)[:, 0] - 1
    padded = (counts + blk - 1) // blk * blk
    pad_end = jnp.cumsum(padded)
    pad_start = pad_end - padded
    dest = (pad_start[flat_e] + rank).astype(jnp.int32)
    row_tok = jnp.zeros((p,), jnp.int32).at[dest].set(jnp.arange(m, dtype=jnp.int32) // TOP_K)
    row_gate = jnp.zeros((p,), F32).at[dest].set(gates.reshape(m))
    blk_exp = jnp.minimum(jnp.searchsorted(pad_end, jnp.arange(n_blocks, dtype=jnp.int32) * blk, side='right'),
                          n_exp - 1).astype(jnp.int32)
    return row_tok, row_gate, blk_exp, dest.reshape(t, TOP_K)


def _rope_tables(pos, d_rope):
    half = d_rope // 2
    inv = 1.0 / (ROPE_THETA ** (jnp.arange(half, dtype=F32) * 2.0 / d_rope))
    ang = pos.astype(F32)[:, None] * inv[None, :]
    pad = jnp.zeros((pos.shape[0], LANES - d_rope), F32)
    cos = jnp.concatenate([jnp.cos(ang), jnp.cos(ang), pad], axis=1)
    sin = jnp.concatenate([jnp.sin(ang), jnp.sin(ang), pad], axis=1)
    return cos, sin


def _rel_bucket(rel):
    nb = N_BUCKETS // 2
    max_exact = nb // 2
    ret = jnp.where(rel > 0, nb, 0)
    n = jnp.abs(rel)
    nf = jnp.maximum(n, 1).astype(F32)
    large = max_exact + (jnp.log(nf / max_exact) / math.log(REL_MAX_DIST / max_exact) * (nb - max_exact)).astype(jnp.int32)
    large = jnp.minimum(large, nb - 1)
    return (ret + jnp.where(n < max_exact, n, large)).astype(jnp.int32)


def _bias_table(rel_bias, q_pos, k_pos):
    bias = rel_bias[_rel_bucket(k_pos[None, :] - q_pos[:, None])]
    vis = (k_pos[None, :] // CHUNK) <= (q_pos[:, None] // CHUNK)
    return jnp.transpose(jnp.where(vis[:, :, None], bias, NEG_INF), (2, 0, 1)).astype(F32)


def _rot_half_cols(w):
    half = w.shape[-1] // 2
    return jnp.concatenate([-w[..., half:], w[..., :half]], axis=-1)


def _pad_cols(w, n):
    return jnp.pad(w, [(0, 0)] * (w.ndim - 1) + [(0, n - w.shape[-1])])


def kernel(x_prompt, x_sample, cache_conv, cache_mla_ckv, cache_mla_kpe, cache_diff_k, cache_diff_v, w_in, w_dw, b_dw, cn_g, cn_b, w_a_out, qn_g, w_uq, kvn_g, w_ukv, w_b_out, lam_q1, lam_k1, lam_q2, lam_k2, sub_g, w_c_out, rel_bias, w_out, ln1_g, ln1_b, w_router, b_router, w_gu, b_gu, w_dn, b_dn, ln2_g, ln2_b):
    nbp, sp, d = x_prompt.shape
    nbd, sd, _ = x_sample.shape
    depth = w_in.shape[0]
    past = cache_mla_ckv.shape[2]
    skd = past + sd
    taps, c_conv = w_dw.shape[1], w_dw.shape[2]
    q_lora, kv_lora = qn_g.shape[1], kvn_g.shape[1]
    h_b, d_qk = w_uq.shape[2], w_uq.shape[3]
    d_rope = cache_mla_kpe.shape[3]
    d_nope = d_qk - d_rope
    d_vh = w_ukv.shape[3] - d_nope
    h_c, d_hc = cache_diff_k.shape[3], cache_diff_k.shape[4]
    n_exp = w_gu.shape[1]
    dn_alpha = (2 * depth) ** 0.25
    mla_scale = d_qk ** -0.5
    diff_scale = (d_hc // 2) ** -0.5
    assert d_nope == LANES and d_vh == LANES and d_hc == LANES and d_rope <= LANES
    assert past % CHUNK == 0 and sd <= CHUNK

    tp, td = nbp * sp, nbd * sd
    t_all = tp + td
    tm = _pick(t_all, MM_TM, BF16_SUBLANES)
    tm_p = _pick(tp, MM_TM, BF16_SUBLANES)
    tm_kd = _pick(nbd * skd, MM_TM, BF16_SUBLANES)
    halo = -(-(taps - 1) // 8) * 8

    pos_p = jnp.arange(sp, dtype=jnp.int32)
    pos_d = past + jnp.arange(sd, dtype=jnp.int32)
    pos_all = jnp.concatenate([jnp.tile(pos_p, nbp), jnp.tile(pos_d, nbd)])
    cos_all, sin_all = _rope_tables(pos_all, d_rope)
    tq_p = _pick(sp, ATT_TQ, BF16_SUBLANES)
    if sp > tq_p:
        nd = sp // tq_p
        bias_p = jnp.stack([_bias_table(rel_bias, dd * tq_p + jnp.arange(tq_p, dtype=jnp.int32),
                                        jnp.arange(tq_p, dtype=jnp.int32)) for dd in range(nd)], axis=1)
    else:
        bias_p = _bias_table(rel_bias, pos_p, pos_p)[:, None]
    bias_d = _bias_table(rel_bias, pos_d, jnp.arange(skd, dtype=jnp.int32))[:, None]

    sizes = [c_conv, c_conv, q_lora, kv_lora, d_rope, h_c * d_hc, h_c * d_hc, h_c * d_hc, 3 * d]
    offs = [0] + [int(v) for v in np.cumsum(sizes)]

    x = jnp.concatenate([x_prompt.reshape(tp, d), x_sample.reshape(td, d)], axis=0)
    xb = x.astype(BF16)
    tile_ij = lambda i, j: (i, j)
    tile_i0 = lambda i, j: (i, 0)
    row_0j = lambda i, j: (0, j)
    states = [[] for _ in range(10)]

    for layer in range(depth):
        lam_init = 0.8 - 0.6 * math.exp(-0.3 * layer)
        wl = w_in[layer]
        seg = [wl[:, offs[k]:offs[k + 1]] for k in range(len(sizes))]
        w_av, w_ag, w_cq, w_ckv = (s.astype(BF16) for s in seg[:4])
        w_kpe = _pad_cols(seg[4], LANES).astype(BF16)
        w_kpe_rot = _pad_cols(_rot_half_cols(seg[4]), LANES).astype(BF16)
        w_dqkv = jnp.concatenate(seg[5:8], axis=1).astype(BF16)
        w_gate = seg[8].astype(BF16)

        tn = _pick(c_conv, MM_TN, LANES)
        glu, = fused_matmul([xb], [(0, w_av, tn), (0, w_ag, tn)], [],
                            lambda a, e: [a[0] * jax.nn.sigmoid(a[1])], [(c_conv, tn, F32)],
                            m_rows=t_all, n_tiles=c_conv // tn, tm=tm, name="in_glu")
        hq, = fused_matmul([xb], [(0, w_cq, q_lora)], [(qn_g[layer].reshape(1, q_lora), (1, q_lora), row_0j)],
                           lambda a, e: [_rms(a[0], e[0])], [(q_lora, q_lora, BF16)],
                           m_rows=t_all, n_tiles=1, tm=tm, name="in_cq")
        ckv, ckv_b = fused_matmul([xb], [(0, w_ckv, kv_lora)],
                                  [(kvn_g[layer].reshape(1, kv_lora), (1, kv_lora), row_0j)],
                                  lambda a, e: [_rms(a[0], e[0])] * 2,
                                  [(kv_lora, kv_lora, F32), (kv_lora, kv_lora, BF16)],
                                  m_rows=t_all, n_tiles=1, tm=tm, name="in_ckv")
        kpe, = fused_matmul([xb], [(0, w_kpe, LANES), (0, w_kpe_rot, LANES)],
                            [(cos_all, (tm, LANES), tile_i0), (sin_all, (tm, LANES), tile_i0)],
                            lambda a, e: [a[0] * e[0] + a[1] * e[1]], [(LANES, LANES, F32)],
                            m_rows=t_all, n_tiles=1, tm=tm, name="in_kpe")
        n_qkv = 3 * h_c * d_hc
        tn = _pick(h_c * d_hc, MM_TN, LANES)
        nq_t = h_c * d_hc // tn

        def qkv_epilogue(a, e, nq_t=nq_t):
            j = pl.program_id(1)
            return [a[0], jnp.where(j < nq_t, a[0] * diff_scale, a[0])]

        dqkv, dqkv_b = fused_matmul([xb], [(0, w_dqkv, tn)], [], qkv_epilogue,
                                    [(n_qkv, tn, F32), (n_qkv, tn, BF16)],
                                    m_rows=t_all, n_tiles=n_qkv // tn, tm=tm, name="in_dqkv")
        tn = _pick(3 * d, MM_TN, LANES)
        gates, = fused_matmul([xb], [(0, w_gate, tn)], [], lambda a, e: [jax.nn.sigmoid(a[0])],
                              [(3 * d, tn, BF16)], m_rows=t_all, n_tiles=3 * d // tn, tm=tm, name="in_gates")

        hist_p = jnp.zeros((nbp, halo, c_conv), F32)
        hist_d = jnp.pad(cache_conv[layer], ((0, 0), (halo - (taps - 1), 0), (0, 0)))
        conv_args = (w_dw[layer], b_dw[layer], cn_g[layer], cn_b[layer])
        ca = jnp.concatenate([conv_module(glu, 0, nbp, sp, hist_p, *conv_args),
                              conv_module(glu, tp, nbd, sd, hist_d, *conv_args)], axis=0)
        glu_p = glu[:tp].reshape(nbp, sp, c_conv)
        glu_d = glu[tp:].reshape(nbd, sd, c_conv)
        new_conv_p = jnp.concatenate([jnp.zeros((nbp, taps - 1, c_conv), F32), glu_p], axis=1)[:, -(taps - 1):]
        new_conv_d = jnp.concatenate([cache_conv[layer], glu_d], axis=1)[:, -(taps - 1):]

        w1 = _pad_cols(w_uq[layer], 2 * LANES).reshape(q_lora, h_b * 2 * LANES).astype(BF16)
        w2 = _pad_cols(_rot_half_cols(w_uq[layer][..., d_nope:]), LANES).reshape(q_lora, h_b * LANES).astype(BF16)

        def q_epilogue(a, e):
            rope = a[0][:, LANES:] * e[0] + a[1] * e[1]
            return [jnp.concatenate([a[0][:, :LANES], rope], axis=1) * mla_scale]

        q_mla, = fused_matmul([hq], [(0, w1, 2 * LANES), (0, w2, LANES)],
                              [(cos_all, (tm, LANES), tile_i0), (sin_all, (tm, LANES), tile_i0)],
                              q_epilogue, [(h_b * 2 * LANES, 2 * LANES, BF16)],
                              m_rows=t_all, n_tiles=h_b, tm=tm, name="mla_q")
        w_kv = w_ukv[layer].reshape(kv_lora, h_b * (d_nope + d_vh)).astype(BF16)
        kpe_b = kpe.astype(BF16)

        def kv_epilogue(a, e):
            return [jnp.concatenate([a[0][:, :LANES], e[0].astype(F32)], axis=1), a[0][:, LANES:]]

        kv_outs = [(h_b * 2 * LANES, 2 * LANES, BF16), (h_b * d_vh, d_vh, BF16)]
        k_p, v_p = fused_matmul([ckv_b], [(0, w_kv, 2 * LANES)], [(kpe_b, (tm_p, LANES), tile_i0)],
                                kv_epilogue, kv_outs, m_rows=tp, n_tiles=h_b, tm=tm_p, name="mla_kv_prompt")
        ckv_hist = jnp.concatenate([cache_mla_ckv[layer].astype(BF16), ckv_b[tp:].reshape(nbd, sd, kv_lora)],
                                   axis=1).reshape(nbd * skd, kv_lora)
        kpe_hist = jnp.concatenate([_pad_cols(cache_mla_kpe[layer], LANES).astype(BF16),
                                    kpe_b[tp:].reshape(nbd, sd, LANES)], axis=1).reshape(nbd * skd, LANES)
        k_d, v_d = fused_matmul([ckv_hist], [(0, w_kv, 2 * LANES)], [(kpe_hist, (tm_kd, LANES), tile_i0)],
                                kv_epilogue, kv_outs, m_rows=nbd * skd, n_tiles=h_b, tm=tm_kd, name="mla_kv_decode")
        ob = jnp.concatenate([
            mla_attention(q_mla, k_p, v_p, nb=nbp, nh=h_b, sq=sp, sk=sp, q_off=0, q_row_off=0),
            mla_attention(q_mla, k_d, v_d, nb=nbd, nh=h_b, sq=sd, sk=skd, q_off=past, q_row_off=tp)], axis=0)

        n_c = h_c * d_hc
        lam_vecs = (lam_q1[layer], lam_k1[layer], lam_q2[layer], lam_k2[layer])
        dq_b, dk_b, dv_b = dqkv_b[:, :n_c], dqkv_b[:, n_c:2 * n_c], dqkv_b[:, 2 * n_c:]
        dk_hist = jnp.concatenate([cache_diff_k[layer].reshape(nbd, past, n_c).astype(BF16),
                                   dk_b[tp:].reshape(nbd, sd, n_c)], axis=1).reshape(nbd * skd, n_c)
        dv_hist = jnp.concatenate([cache_diff_v[layer].reshape(nbd, past, n_c).astype(BF16),
                                   dv_b[tp:].reshape(nbd, sd, n_c)], axis=1).reshape(nbd * skd, n_c)
        oc = jnp.concatenate([
            diff_attention(dq_b, dk_b, dv_b, bias_p, lam_vecs, sub_g[layer], lam_init,
                           nb=nbp, nh=h_c, sq=sp, sk=sp, q_off=0, q_row_off=0),
            diff_attention(dq_b, dk_hist, dv_hist, bias_d, lam_vecs, sub_g[layer], lam_init,
                           nb=nbd, nh=h_c, sq=sd, sk=skd, q_off=past, q_row_off=tp)], axis=0)

        tn = _pick(d, MM_TN, LANES)
        nt = d // tn
        merged, = fused_matmul(
            [ca, ob, oc],
            [(0, w_a_out[layer].astype(BF16), tn), (1, w_b_out[layer].astype(BF16), tn),
             (2, w_c_out[layer].astype(BF16), tn)],
            [(gates, (tm, tn), tile_ij), (gates, (tm, tn), lambda i, j, nt=nt: (i, j + nt)),
             (gates, (tm, tn), lambda i, j, nt=nt: (i, j + 2 * nt))],
            lambda a, e: [e[0].astype(F32) * a[0] + e[1].astype(F32) * a[1] + e[2].astype(F32) * a[2]],
            [(d, tn, BF16)], m_rows=t_all, n_tiles=nt, tm=tm, name="merge")
        t1, = fused_matmul([merged], [(0, w_out[layer].astype(BF16), tn)], [(x, (tm, tn), tile_ij)],
                           lambda a, e: [dn_alpha * e[0] + a[0]], [(d, tn, F32)],
                           m_rows=t_all, n_tiles=nt, tm=tm, name="w_out")
        x1, x1b = layernorm_rows(t1, ln1_g[layer], ln1_b[layer])

        w_r = _pad_cols(w_router[layer], LANES).astype(BF16)
        b_r = jnp.concatenate([b_router[layer], jnp.full((LANES - n_exp,), NEG_INF, F32)]).reshape(1, LANES)
        top_i, top_g = fused_matmul([x1b], [(0, w_r, LANES)], [(b_r, (1, LANES), row_0j)], _router_epilogue,
                                    [(LANES, LANES, jnp.int32), (LANES, LANES, F32)],
                                    m_rows=t_all, n_tiles=1, tm=tm, name="router")
        row_tok, row_gate, blk_exp, dest = _route_plan(top_i[:, :TOP_K], top_g[:, :TOP_K], n_exp, MOE_BLK)
        xs = gather_rows(x1, row_tok, MOE_BLK)
        y_rows = expert_ffn(xs, blk_exp, row_gate, w_gu[layer].astype(BF16), b_gu[layer],
                            w_dn[layer].astype(BF16), b_dn[layer], MOE_BLK)
        x, xb = moe_combine_ln(y_rows, dest, x1, ln2_g[layer], ln2_b[layer], dn_alpha)

        dqkv_p = dqkv[:tp]
        dqkv_d = dqkv[tp:]
        outs = (new_conv_p, ckv[:tp].reshape(nbp, sp, kv_lora), kpe[:tp, :d_rope].reshape(nbp, sp, d_rope),
                dqkv_p[:, n_c:2 * n_c].reshape(nbp, sp, h_c, d_hc), dqkv_p[:, 2 * n_c:].reshape(nbp, sp, h_c, d_hc),
                new_conv_d, ckv[tp:].reshape(nbd, sd, kv_lora), kpe[tp:, :d_rope].reshape(nbd, sd, d_rope),
                dqkv_d[:, n_c:2 * n_c].reshape(nbd, sd, h_c, d_hc), dqkv_d[:, 2 * n_c:].reshape(nbd, sd, h_c, d_hc))
        for buf, s in zip(states, outs):
            buf.append(s)

    y_prompt = x[:tp].reshape(nbp, sp, d)
    y_sample = x[tp:].reshape(nbd, sd, d)
    return (y_prompt, y_sample) + tuple(jnp.stack(b) for b in states)
```

```python
import functools
import math

import numpy as np
import jax
import jax.numpy as jnp
from jax import lax
from jax.experimental import pallas as pl
from jax.experimental.pallas import tpu as pltpu

BF16 = jnp.bfloat16
F32 = jnp.float32
U32 = jnp.uint32
I32 = jnp.int32

CHUNK = 64
ROPE_THETA = 10000.0
N_BUCKETS = 32
REL_MAX_DIST = 1024
TOP_K = 4
SWIGLU_LIMIT = 7.0
SWIGLU_ALPHA = 1.702
LN_EPS = 1e-5
NEG_INF = -1e30

VMEM_LIMIT_BYTES = 56 * 1024 * 1024
LANES = 128
BF16_SUBLANES = 16

MM_TM = 1280
MM_TN = 512
ATT_TQ = 256
ATT_TK = 256
ATT_SINGLE_PASS_SQ = 64
MLA_DECODE_HEADS = 4
DIFF_DECODE_HEADS = 8
CONV_TS = 256
CONV_ROWS = 32
LN_TM = 256
MOE_BLK = 256
DISPATCH_TC = 256
COMBINE_TC = 256
DMA_UNROLL = 8


def _pick(n, target, mult):
    best = None
    for d in range(mult, min(n, target) + 1, mult):
        if n % d == 0:
            best = d
    return n if best is None else best


def _params(*sem):
    return pltpu.CompilerParams(dimension_semantics=sem, vmem_limit_bytes=VMEM_LIMIT_BYTES)


def _fused_mm_kernel(*refs, pair_x, n_x, n_w, n_e, epilogue, prologue):
    x_refs = refs[:n_x]
    w_refs = refs[n_x:n_x + n_w]
    e_refs = refs[n_x + n_w:n_x + n_w + n_e]
    o_refs = refs[n_x + n_w + n_e:]
    extras = [e[...] for e in e_refs]
    xs = [x[...] for x in x_refs]
    if prologue is not None:
        xs = prologue(xs, extras)
    accs = [jnp.dot(xs[xi], w_refs[k][...], preferred_element_type=F32) for k, xi in enumerate(pair_x)]
    outs = epilogue(accs, extras)
    for o_ref, o in zip(o_refs, outs):
        o_ref[...] = o.astype(o_ref.dtype)


def fused_matmul(xs, pairs, extras, epilogue, outs, *, layer, m_rows, n_tiles, tm, prologue=None, name=None):
    assert m_rows % tm == 0
    grid = (m_rows // tm, n_tiles)
    in_specs = []
    for x in xs:
        in_specs.append(pl.BlockSpec((tm, x.shape[1]), lambda i, j: (i, 0)))
    for _, w, wn, off in pairs:
        in_specs.append(pl.BlockSpec((None, w.shape[1], wn), lambda i, j, off=off: (layer, 0, j + off)))
    for _, bshape, imap in extras:
        in_specs.append(pl.BlockSpec(bshape, imap))
    out_shape = [jax.ShapeDtypeStruct((m_rows, n), dt) for n, _, dt in outs]
    out_specs = [pl.BlockSpec((tm, bn), lambda i, j: (i, j)) for _, bn, _ in outs]
    kern = functools.partial(_fused_mm_kernel, pair_x=tuple(p[0] for p in pairs), n_x=len(xs),
                             n_w=len(pairs), n_e=len(extras), epilogue=epilogue, prologue=prologue)
    return pl.pallas_call(
        kern, grid=grid, in_specs=in_specs, out_specs=out_specs, out_shape=out_shape,
        compiler_params=_params("parallel", "arbitrary"), name=name,
    )(*xs, *[p[1] for p in pairs], *[e[0] for e in extras])


def _rms(x, g):
    return x * lax.rsqrt(jnp.mean(x * x, axis=-1, keepdims=True) + LN_EPS) * g


def _layernorm(x, g, b):
    mu = jnp.mean(x, axis=-1, keepdims=True)
    xc = x - mu
    var = jnp.mean(xc * xc, axis=-1, keepdims=True)
    return xc * lax.rsqrt(var + LN_EPS) * g + b


def _pack_bf16_pairs(x):
    h = x.shape[1] // 2
    lo = lax.bitcast_convert_type(x[:, :h].astype(BF16).astype(F32), U32)
    hi = lax.bitcast_convert_type(x[:, h:].astype(BF16).astype(F32), U32)
    return lax.shift_right_logical(lo, jnp.uint32(16)) | (hi & jnp.uint32(0xFFFF0000))


def _unpack_bf16_pairs(p):
    lo = lax.bitcast_convert_type(lax.shift_left(p, jnp.uint32(16)), F32).astype(BF16)
    hi = lax.bitcast_convert_type(p & jnp.uint32(0xFFFF0000), F32).astype(BF16)
    return lo, hi


def _ln_kernel(t_ref, g_ref, b_ref, o_ref, ob_ref, op_ref):
    y = _layernorm(t_ref[...], g_ref[...], b_ref[...])
    o_ref[...] = y
    ob_ref[...] = y.astype(BF16)
    op_ref[...] = _pack_bf16_pairs(y)


def layernorm_rows(t, g, b, layer):
    m, d = t.shape
    tm = _pick(m, LN_TM, BF16_SUBLANES)
    vec = pl.BlockSpec((None, 1, d), lambda i: (layer, 0, 0))
    row = lambda n: pl.BlockSpec((tm, n), lambda i: (i, 0))
    return pl.pallas_call(
        _ln_kernel, grid=(m // tm,),
        in_specs=[row(d), vec, vec],
        out_specs=[row(d), row(d), row(d // 2)],
        out_shape=[jax.ShapeDtypeStruct((m, d), F32), jax.ShapeDtypeStruct((m, d), BF16),
                   jax.ShapeDtypeStruct((m, d // 2), U32)],
        compiler_params=_params("parallel"), name="layernorm_rows",
    )(t, g, b)


def _conv_kernel(prev_ref, hist_ref, x_ref, w_ref, bdw_ref, g_ref, b_ref, o_ref, win_ref, *, ts, halo, taps, rows):
    @pl.when(pl.program_id(1) == 0)
    def _():
        win_ref[0:halo, :] = hist_ref[0]

    @pl.when(pl.program_id(1) != 0)
    def _():
        win_ref[0:halo, :] = prev_ref[...]

    win_ref[halo:halo + ts, :] = x_ref[...]
    lead = halo - (taps - 1)
    for r0 in range(0, ts, rows):
        acc = jnp.zeros((rows, x_ref.shape[1]), F32) + bdw_ref[...]
        for k in range(taps):
            acc = acc + w_ref[k:k + 1, :] * win_ref[r0 + lead + k:r0 + lead + k + rows, :]
        y = _layernorm(acc, g_ref[...], b_ref[...])
        o_ref[r0:r0 + rows, :] = (y * jax.nn.sigmoid(y)).astype(o_ref.dtype)


def conv_module(glu, row_off, nb, s, hist, w_dw, b_dw, cn_g, cn_b, layer):
    c = glu.shape[1]
    taps = w_dw.shape[1]
    halo = hist.shape[1]
    ts = _pick(s, CONV_TS, halo)
    rows = _pick(ts, CONV_ROWS, 8)
    nblk = s // ts
    assert row_off % ts == 0 and ts % halo == 0
    kern = functools.partial(_conv_kernel, ts=ts, halo=halo, taps=taps, rows=rows)
    off_main = row_off // ts
    off_prev = row_off // halo
    per = ts // halo
    vec = pl.BlockSpec((None, 1, c), lambda b, i: (layer, 0, 0))
    return pl.pallas_call(
        kern, grid=(nb, nblk),
        in_specs=[
            pl.BlockSpec((halo, c), lambda b, i: (off_prev + jnp.maximum((b * nblk + i) * per - 1, 0), 0)),
            pl.BlockSpec((1, halo, c), lambda b, i: (b, 0, 0)),
            pl.BlockSpec((ts, c), lambda b, i: (off_main + b * nblk + i, 0)),
            pl.BlockSpec((None, taps, c), lambda b, i: (layer, 0, 0)),
            vec, vec, vec,
        ],
        out_specs=pl.BlockSpec((ts, c), lambda b, i: (b * nblk + i, 0)),
        out_shape=jax.ShapeDtypeStruct((nb * s, c), BF16),
        scratch_shapes=[pltpu.VMEM((halo + ts, c), F32)],
        compiler_params=_params("parallel", "arbitrary"), name="conv_module",
    )(glu, hist, glu, w_dw, b_dw, cn_g, cn_b)


def _block_plan(sq, sk, q_off, tq, tk):
    plan = []
    for q0 in range(0, sq, tq):
        k_any = min(sk, ((q_off + q0 + tq - 1) // CHUNK + 1) * CHUNK)
        k_all = min(sk, ((q_off + q0) // CHUNK + 1) * CHUNK)
        blocks = []
        for k0 in range(0, k_any, tk):
            kl = min(tk, k_any - k0)
            blocks.append((k0, kl, k0 + kl > k_all))
        plan.append((q0, blocks))
    return plan


def _online_step(s, v, m, l, acc):
    m_new = jnp.maximum(m, jnp.max(s, axis=-1, keepdims=True))
    a = jnp.exp(m - m_new)
    p = jnp.exp(s - m_new)
    l = a * l + jnp.sum(p, axis=-1, keepdims=True)
    acc = a * acc + jnp.dot(p.astype(v.dtype), v, preferred_element_type=F32)
    return m_new, l, acc


def _qk(q, k):
    return lax.dot_general(q, k, (((1,), (1,)), ((), ())), preferred_element_type=F32)


def _att_tiles(sq, sk):
    tq = _pick(sq, ATT_TQ, BF16_SUBLANES)
    if sq <= ATT_SINGLE_PASS_SQ:
        return tq, sk
    tk = _pick(sk, ATT_TK, BF16_SUBLANES) if sk % ATT_TK else ATT_TK
    return tq, tk


def _head_group(nh, sq, target):
    return math.gcd(nh, target) if sq <= ATT_SINGLE_PASS_SQ else 1


def _mla_kernel(q_ref, k_ref, v_ref, o_ref, *, plan, tq, q_off, hg):
    dq = q_ref.shape[1] // hg
    dv = v_ref.shape[1] // hg
    for h in range(hg):
        for q0, blocks in plan:
            q = q_ref[q0:q0 + tq, h * dq:(h + 1) * dq]
            m = jnp.full((tq, 1), -jnp.inf, F32)
            l = jnp.zeros((tq, 1), F32)
            acc = jnp.zeros((tq, dv), F32)
            for k0, kl, masked in blocks:
                s = _qk(q, k_ref[k0:k0 + kl, h * dq:(h + 1) * dq])
                if masked:
                    qp = q_off + q0 + lax.broadcasted_iota(I32, (tq, kl), 0)
                    kp = k0 + lax.broadcasted_iota(I32, (tq, kl), 1)
                    s = jnp.where(kp // CHUNK <= qp // CHUNK, s, NEG_INF)
                m, l, acc = _online_step(s, v_ref[k0:k0 + kl, h * dv:(h + 1) * dv], m, l, acc)
            o_ref[q0:q0 + tq, h * dv:(h + 1) * dv] = (acc / l).astype(o_ref.dtype)


def mla_attention(q, k, v, *, nb, nh, sq, sk, q_off, q_row_off):
    dq = q.shape[1] // nh
    dv = v.shape[1] // nh
    tq, tk = _att_tiles(sq, sk)
    hg = _head_group(nh, sq, MLA_DECODE_HEADS)
    plan = _block_plan(sq, sk, q_off, tq, tk)
    assert q_row_off % sq == 0
    qb = q_row_off // sq
    kern = functools.partial(_mla_kernel, plan=plan, tq=tq, q_off=q_off, hg=hg)
    return pl.pallas_call(
        kern, grid=(nb, nh // hg),
        in_specs=[pl.BlockSpec((sq, hg * dq), lambda b, h: (qb + b, h)),
                  pl.BlockSpec((sk, hg * dq), lambda b, h: (b, h)),
                  pl.BlockSpec((sk, hg * dv), lambda b, h: (b, h))],
        out_specs=pl.BlockSpec((sq, hg * dv), lambda b, h: (b, h)),
        out_shape=jax.ShapeDtypeStruct((nb * sq, nh * dv), BF16),
        compiler_params=_params("parallel", "parallel"), name="mla_attention",
    )(q, k, v)


def _diff_kernel(q_ref, k_ref, v_ref, bias_ref, lq1_ref, lk1_ref, lq2_ref, lk2_ref, sg_ref, o_ref, *,
                 plan, tq, tk, lam_init, bias_blocked, hg):
    dh = LANES
    half = dh // 2
    lam = (jnp.exp(jnp.sum(lq1_ref[...] * lk1_ref[...], axis=-1, keepdims=True))
           - jnp.exp(jnp.sum(lq2_ref[...] * lk2_ref[...], axis=-1, keepdims=True)) + lam_init)
    lane = lax.broadcasted_iota(I32, (tq, dh), 1)
    for h in range(hg):
        cols = slice(h * dh, (h + 1) * dh)
        for q0, blocks in plan:
            q = q_ref[q0:q0 + tq, cols].astype(F32)
            q2 = jnp.concatenate([jnp.where(lane < half, q, 0.0), jnp.where(lane >= half, q, 0.0)],
                                 axis=0).astype(q_ref.dtype)
            m = jnp.full((2 * tq, 1), -jnp.inf, F32)
            l = jnp.zeros((2 * tq, 1), F32)
            acc = jnp.zeros((2 * tq, dh), F32)
            for k0, kl, _ in blocks:
                if bias_blocked:
                    bias = bias_ref[h, (q0 - k0) // tk]
                else:
                    bias = bias_ref[h, 0, :, k0:k0 + kl]
                s = _qk(q2, k_ref[k0:k0 + kl, cols]) + jnp.concatenate([bias, bias], axis=0)
                m, l, acc = _online_step(s, v_ref[k0:k0 + kl, cols], m, l, acc)
            o = acc / l
            o = o[:tq] - lam * o[tq:]
            o = _rms(o, sg_ref[...]) * (1.0 - lam_init)
            o_ref[q0:q0 + tq, cols] = o.astype(o_ref.dtype)


def diff_attention(q, k, v, bias, lam_vecs, sub_g, lam_init, layer, *, nb, nh, sq, sk, q_off, q_row_off,
                   q_col, k_col, v_col):
    dh = LANES
    tq, tk = _att_tiles(sq, sk)
    hg = _head_group(nh, sq, DIFF_DECODE_HEADS)
    assert q_col % hg == 0 and k_col % hg == 0 and v_col % hg == 0
    plan = _block_plan(sq, sk, q_off, tq, tk)
    bias_blocked = bias.shape[2:] == (tq, tk) and sq > tq
    qb = q_row_off // sq
    kern = functools.partial(_diff_kernel, plan=plan, tq=tq, tk=tk, lam_init=lam_init, bias_blocked=bias_blocked,
                             hg=hg)
    vec = lambda n: pl.BlockSpec((None, 1, n), lambda b, h: (layer, 0, 0))
    return pl.pallas_call(
        kern, grid=(nb, nh // hg),
        in_specs=[pl.BlockSpec((sq, hg * dh), lambda b, h: (qb + b, q_col // hg + h)),
                  pl.BlockSpec((sk, hg * dh), lambda b, h: (b, k_col // hg + h)),
                  pl.BlockSpec((sk, hg * dh), lambda b, h: (b, v_col // hg + h)),
                  pl.BlockSpec((hg,) + bias.shape[1:], lambda b, h: (h, 0, 0, 0)),
                  vec(dh // 2), vec(dh // 2), vec(dh // 2), vec(dh // 2), vec(dh)],
        out_specs=pl.BlockSpec((sq, hg * dh), lambda b, h: (b, h)),
        out_shape=jax.ShapeDtypeStruct((nb * sq, nh * dh), BF16),
        compiler_params=_params("parallel", "parallel"), name="diff_attention",
    )(q, k, v, bias, *lam_vecs, sub_g)


def _router_kernel(x_ref, w_ref, b_ref, tri_ref, idx_ref, gate_ref, rank_ref, cnt_ref, carry_ref):
    @pl.when(pl.program_id(0) == 0)
    def _():
        carry_ref[...] = jnp.zeros_like(carry_ref)

    logits = jnp.dot(x_ref[...], w_ref[...], preferred_element_type=F32) + b_ref[...]
    n = logits.shape[1]
    lane = lax.broadcasted_iota(I32, logits.shape, 1).astype(F32)
    vals, idxs = [], []
    for _ in range(TOP_K):
        mx = jnp.max(logits, axis=-1, keepdims=True)
        idx = jnp.min(jnp.where(logits == mx, lane, float(n)), axis=-1, keepdims=True)
        vals.append(mx)
        idxs.append(idx)
        logits = jnp.where(lane == idx, -jnp.inf, logits)
    es = [jnp.exp(v - vals[0]) for v in vals]
    den = es[0]
    for e in es[1:]:
        den = den + e
    sel = jnp.zeros(logits.shape, F32)
    for k in range(TOP_K):
        sel = jnp.where(lane == idxs[k], 1.0, sel)
    before = jnp.dot(tri_ref[...], sel.astype(BF16), preferred_element_type=F32) + carry_ref[0:1, :]
    idx_out = jnp.zeros(logits.shape, F32)
    gate_out = jnp.zeros(logits.shape, F32)
    rank_out = jnp.zeros(logits.shape, F32)
    for k in range(TOP_K):
        rank = jnp.sum(jnp.where(lane == idxs[k], before, 0.0), axis=-1, keepdims=True)
        idx_out = jnp.where(lane == float(k), idxs[k], idx_out)
        gate_out = jnp.where(lane == float(k), es[k] / den, gate_out)
        rank_out = jnp.where(lane == float(k), rank, rank_out)
    idx_ref[...] = idx_out.astype(I32)
    gate_ref[...] = gate_out
    rank_ref[...] = rank_out.astype(I32)
    carry_ref[0:1, :] = carry_ref[0:1, :] + jnp.sum(sel, axis=0, keepdims=True)
    cnt_ref[...] = jnp.broadcast_to(carry_ref[0:1, :], cnt_ref.shape).astype(I32)


def router(xb, w_r, b_r, layer, tm):
    m, d = xb.shape
    tri = (np.arange(tm)[:, None] > np.arange(tm)[None, :]).astype(np.float32)
    tri = jnp.asarray(tri, BF16)
    row = pl.BlockSpec((tm, LANES), lambda i: (i, 0))
    return pl.pallas_call(
        _router_kernel, grid=(m // tm,),
        in_specs=[pl.BlockSpec((tm, d), lambda i: (i, 0)),
                  pl.BlockSpec((None, d, LANES), lambda i: (layer, 0, 0)),
                  pl.BlockSpec((None, 1, LANES), lambda i: (layer, 0, 0)),
                  pl.BlockSpec((tm, tm), lambda i: (0, 0))],
        out_specs=[row, row, row, pl.BlockSpec((8, LANES), lambda i: (0, 0))],
        out_shape=[jax.ShapeDtypeStruct((m, LANES), I32), jax.ShapeDtypeStruct((m, LANES), F32),
                   jax.ShapeDtypeStruct((m, LANES), I32), jax.ShapeDtypeStruct((8, LANES), I32)],
        scratch_shapes=[pltpu.VMEM((8, LANES), F32)],
        compiler_params=_params("arbitrary"), name="router",
    )(xb, w_r, b_r, tri)


def _dispatch_kernel(info_ref, dest_ref, x_ref, o_ref, zero_ref, sem, fill_sem, *, n_exp, blk, n_blocks, tc):
    @pl.when(pl.program_id(0) == 0)
    def _():
        zero_ref[...] = jnp.zeros_like(zero_ref)

        def per_expert(e, c):
            first = info_ref[e]
            count = info_ref[n_exp + e]

            def start(r, c2):
                pltpu.make_async_copy(zero_ref.at[pl.ds(0, 1), :], o_ref.at[pl.ds(first + r, 1), :], fill_sem).start()
                return c2

            def wait(r, c2):
                pltpu.make_async_copy(zero_ref.at[pl.ds(0, 1), :], o_ref.at[pl.ds(first, 1), :], fill_sem).wait()
                return c2

            lax.fori_loop(0, count, start, 0)
            lax.fori_loop(0, count, wait, 0)
            return c

        lax.fori_loop(0, n_exp, per_expert, 0)
        used = info_ref[2 * n_exp]

        def tail_start(b, c):
            pltpu.make_async_copy(zero_ref, o_ref.at[pl.ds(b * blk, blk), :], fill_sem).start()
            return c

        def tail_wait(b, c):
            pltpu.make_async_copy(zero_ref, o_ref.at[pl.ds(b * blk, blk), :], fill_sem).wait()
            return c

        lax.fori_loop(used, n_blocks, tail_start, 0)
        lax.fori_loop(used, n_blocks, tail_wait, 0)

    def scatter(r, c):
        for k in range(TOP_K):
            pltpu.make_async_copy(x_ref.at[pl.ds(r, 1), :],
                                  o_ref.at[pl.ds(dest_ref[0, 0, k * tc + r], 1), :], sem).start(priority=k % 2)
        return c

    lax.fori_loop(0, tc, scatter, 0, unroll=DMA_UNROLL)
    for k in range(TOP_K):
        pltpu.make_async_copy(x_ref, o_ref.at[pl.ds(0, tc), :], sem).wait()


def moe_dispatch(xp, dest, fill_info, n_exp, blk, n_blocks):
    t, w = xp.shape
    tc = _pick(t, DISPATCH_TC, 8)
    assert blk % 8 == 0
    nsteps = t // tc
    dest_blocks = dest.reshape(nsteps, tc, TOP_K).transpose(0, 2, 1).reshape(nsteps, 1, TOP_K * tc)
    kern = functools.partial(_dispatch_kernel, n_exp=n_exp, blk=blk, n_blocks=n_blocks, tc=tc)
    return pl.pallas_call(
        kern,
        grid_spec=pltpu.PrefetchScalarGridSpec(
            num_scalar_prefetch=1, grid=(nsteps,),
            in_specs=[pl.BlockSpec((1, 1, TOP_K * tc), lambda i, info: (i, 0, 0), memory_space=pltpu.SMEM),
                      pl.BlockSpec((tc, w), lambda i, info: (i, 0))],
            out_specs=pl.BlockSpec(memory_space=pl.ANY),
            scratch_shapes=[pltpu.VMEM((blk, w), xp.dtype), pltpu.SemaphoreType.DMA(()),
                            pltpu.SemaphoreType.DMA(())]),
        out_shape=jax.ShapeDtypeStruct((n_blocks * blk, w), xp.dtype),
        compiler_params=_params("arbitrary"), name="moe_dispatch",
    )(fill_info, dest_blocks, xp)


def _expert_up_kernel(be_ref, valid_ref, x_ref, w_ref, b_ref, o_ref):
    de = o_ref.shape[1]
    half = w_ref.shape[0] // 2

    @pl.when(valid_ref[pl.program_id(0)] != 0)
    def _():
        lo, hi = _unpack_bf16_pairs(x_ref[...])
        gu = (jnp.dot(lo, w_ref[0:half, :], preferred_element_type=F32)
              + jnp.dot(hi, w_ref[half:, :], preferred_element_type=F32) + b_ref[...])
        g = jnp.minimum(gu[:, :de], SWIGLU_LIMIT)
        up = jnp.clip(gu[:, de:], -SWIGLU_LIMIT, SWIGLU_LIMIT)
        o_ref[...] = (g * jax.nn.sigmoid(SWIGLU_ALPHA * g) * (up + 1.0)).astype(o_ref.dtype)

    @pl.when(valid_ref[pl.program_id(0)] == 0)
    def _():
        o_ref[...] = jnp.zeros_like(o_ref)


def _expert_down_kernel(be_ref, valid_ref, a_ref, w_ref, b_ref, o_ref):
    @pl.when(valid_ref[pl.program_id(0)] != 0)
    def _():
        o_ref[...] = _pack_bf16_pairs(jnp.dot(a_ref[...], w_ref[...], preferred_element_type=F32) + b_ref[...])

    @pl.when(valid_ref[pl.program_id(0)] == 0)
    def _():
        o_ref[...] = jnp.zeros_like(o_ref)


def expert_ffn(xs, blk_exp, blk_valid, w_gu, b_gu, w_dn, b_dn, layer, blk):
    p = xs.shape[0]
    d, n2 = w_gu.shape[2], w_gu.shape[3]
    de = n2 // 2
    nblk = p // blk
    act = pl.pallas_call(
        _expert_up_kernel,
        grid_spec=pltpu.PrefetchScalarGridSpec(
            num_scalar_prefetch=2, grid=(nblk,),
            in_specs=[pl.BlockSpec((blk, d // 2), lambda i, be, va: (i, 0)),
                      pl.BlockSpec((None, None, d, n2), lambda i, be, va: (layer, be[i], 0, 0)),
                      pl.BlockSpec((None, None, 1, n2), lambda i, be, va: (layer, be[i], 0, 0))],
            out_specs=pl.BlockSpec((blk, de), lambda i, be, va: (i, 0))),
        out_shape=jax.ShapeDtypeStruct((p, de), BF16),
        compiler_params=_params("arbitrary"), name="expert_up",
    )(blk_exp, blk_valid, xs, w_gu, b_gu)
    return pl.pallas_call(
        _expert_down_kernel,
        grid_spec=pltpu.PrefetchScalarGridSpec(
            num_scalar_prefetch=2, grid=(nblk,),
            in_specs=[pl.BlockSpec((blk, de), lambda i, be, va: (i, 0)),
                      pl.BlockSpec((None, None, de, d), lambda i, be, va: (layer, be[i], 0, 0)),
                      pl.BlockSpec((None, None, 1, d), lambda i, be, va: (layer, be[i], 0, 0))],
            out_specs=pl.BlockSpec((blk, d // 2), lambda i, be, va: (i, 0))),
        out_shape=jax.ShapeDtypeStruct((p, d // 2), U32),
        compiler_params=_params("arbitrary"), name="expert_down",
    )(blk_exp, blk_valid, act, w_dn, b_dn)


def _combine_kernel(dest_ref, dest_next_ref, y_ref, x_ref, gate_ref, g_ref, b_ref, o_ref, ob_ref, buf, sem, *,
                    alpha, tc):
    i = pl.program_id(0)
    n = TOP_K * tc

    def issue(d_ref, slot):
        def start(r2, c):
            for j in range(2):
                r = 2 * r2 + j
                pltpu.make_async_copy(y_ref.at[pl.ds(d_ref[0, 0, r], 1), :], buf.at[slot, pl.ds(r, 1), :],
                                      sem.at[slot]).start(priority=j)
            return c
        lax.fori_loop(0, n // 2, start, 0, unroll=DMA_UNROLL // 2)

    @pl.when(i == 0)
    def _():
        issue(dest_ref, 0)

    for slot in range(2):
        @pl.when(i % 2 == slot)
        def _(slot=slot):
            @pl.when(i + 1 < pl.num_programs(0))
            def _():
                issue(dest_next_ref, 1 - slot)

            pltpu.make_async_copy(y_ref.at[pl.ds(0, n), :], buf.at[slot], sem.at[slot]).wait()
            gates = gate_ref[...]
            y_lo = y_hi = None
            for k in range(TOP_K):
                p = buf[slot, k * tc:(k + 1) * tc, :]
                lo = gates[:, k:k + 1] * lax.bitcast_convert_type(lax.shift_left(p, jnp.uint32(16)), F32)
                hi = gates[:, k:k + 1] * lax.bitcast_convert_type(p & jnp.uint32(0xFFFF0000), F32)
                y_lo = lo if y_lo is None else y_lo + lo
                y_hi = hi if y_hi is None else y_hi + hi
            y = jnp.concatenate([y_lo, y_hi], axis=1)
            out = _layernorm(alpha * x_ref[...] + y, g_ref[...], b_ref[...])
            o_ref[...] = out
            ob_ref[...] = out.astype(BF16)


def moe_combine_ln(y_rows, dest, gates, x, g, b, layer, alpha):
    t, d = x.shape
    tc = _pick(t, COMBINE_TC, BF16_SUBLANES)
    nblk = t // tc
    dest_blocks = dest.reshape(nblk, tc, TOP_K).transpose(0, 2, 1).reshape(nblk, 1, TOP_K * tc)
    kern = functools.partial(_combine_kernel, alpha=alpha, tc=tc)
    vec = pl.BlockSpec((None, 1, d), lambda i: (layer, 0, 0))
    row = pl.BlockSpec((tc, d), lambda i: (i, 0))
    return pl.pallas_call(
        kern, grid=(nblk,),
        in_specs=[pl.BlockSpec((1, 1, TOP_K * tc), lambda i: (i, 0, 0), memory_space=pltpu.SMEM),
                  pl.BlockSpec((1, 1, TOP_K * tc), lambda i: (jnp.minimum(i + 1, nblk - 1), 0, 0),
                               memory_space=pltpu.SMEM),
                  pl.BlockSpec(memory_space=pl.ANY),
                  row, pl.BlockSpec((tc, LANES), lambda i: (i, 0)), vec, vec],
        out_specs=[row, row],
        out_shape=[jax.ShapeDtypeStruct((t, d), F32), jax.ShapeDtypeStruct((t, d), BF16)],
        scratch_shapes=[pltpu.VMEM((2, TOP_K * tc, d // 2), U32), pltpu.SemaphoreType.DMA((2,))],
        compiler_params=_params("arbitrary"), name="moe_combine_ln",
    )(dest_blocks, dest_blocks, y_rows, x, gates, g, b)


def _route_plan(idx, rank, counts, n_exp, blk, n_blocks):
    padded = (counts + blk - 1) // blk * blk
    pad_end = jnp.cumsum(padded)
    pad_start = pad_end - padded
    experts = jnp.arange(n_exp, dtype=I32)
    dest = rank + jnp.sum(jnp.where(idx[:, :, None] == experts, pad_start, 0), axis=-1)
    block_row0 = jnp.arange(n_blocks, dtype=I32) * blk
    blk_exp = jnp.minimum(jnp.sum((pad_end[None, :] <= block_row0[:, None]).astype(I32), axis=1), n_exp - 1)
    blk_valid = (block_row0 < pad_end[-1]).astype(I32)
    fill_info = jnp.concatenate([pad_start + counts, padded - counts, pad_end[-1:] // blk]).astype(I32)
    return dest.astype(I32), blk_exp.astype(I32), blk_valid, fill_info


def _rope_tables(pos, d_rope):
    half = d_rope // 2
    inv = 1.0 / (ROPE_THETA ** (jnp.arange(half, dtype=F32) * 2.0 / d_rope))
    ang = pos.astype(F32)[:, None] * inv[None, :]
    pad = jnp.zeros((pos.shape[0], LANES - d_rope), F32)
    cos = jnp.concatenate([jnp.cos(ang), jnp.cos(ang), pad], axis=1)
    sin = jnp.concatenate([jnp.sin(ang), jnp.sin(ang), pad], axis=1)
    return cos, sin


def _rel_bucket(rel):
    nb = N_BUCKETS // 2
    max_exact = nb // 2
    ret = jnp.where(rel > 0, nb, 0)
    n = jnp.abs(rel)
    nf = jnp.maximum(n, 1).astype(F32)
    large = max_exact + (jnp.log(nf / max_exact) / math.log(REL_MAX_DIST / max_exact) * (nb - max_exact)).astype(I32)
    large = jnp.minimum(large, nb - 1)
    return (ret + jnp.where(n < max_exact, n, large)).astype(I32)


def _bias_table(rel_bias, q_pos, k_pos):
    bucket = _rel_bucket(k_pos[None, :] - q_pos[:, None])
    vis = (k_pos[None, :] // CHUNK) <= (q_pos[:, None] // CHUNK)
    buckets = jnp.arange(N_BUCKETS, dtype=I32)
    bias = jnp.sum(jnp.where(bucket[None, :, :, None] == buckets, rel_bias.T[:, None, None, :], 0.0), axis=-1)
    return jnp.where(vis[None], bias, NEG_INF).astype(F32)


def _rot_half_cols(w):
    half = w.shape[-1] // 2
    return jnp.concatenate([-w[..., half:], w[..., :half]], axis=-1)


def _pad_cols(w, n):
    return jnp.pad(w, [(0, 0)] * (w.ndim - 1) + [(0, n - w.shape[-1])])


def kernel(x_prompt, x_sample, cache_conv, cache_mla_ckv, cache_mla_kpe, cache_diff_k, cache_diff_v, w_in, w_dw, b_dw, cn_g, cn_b, w_a_out, qn_g, w_uq, kvn_g, w_ukv, w_b_out, lam_q1, lam_k1, lam_q2, lam_k2, sub_g, w_c_out, rel_bias, w_out, ln1_g, ln1_b, w_router, b_router, w_gu, b_gu, w_dn, b_dn, ln2_g, ln2_b):
    nbp, sp, d = x_prompt.shape
    nbd, sd, _ = x_sample.shape
    depth = w_in.shape[0]
    past = cache_mla_ckv.shape[2]
    skd = past + sd
    taps, c_conv = w_dw.shape[1], w_dw.shape[2]
    q_lora, kv_lora = qn_g.shape[1], kvn_g.shape[1]
    h_b, d_qk = w_uq.shape[2], w_uq.shape[3]
    d_rope = cache_mla_kpe.shape[3]
    d_nope = d_qk - d_rope
    d_vh = w_ukv.shape[3] - d_nope
    h_c, d_hc = cache_diff_k.shape[3], cache_diff_k.shape[4]
    n_exp, d_exp = w_gu.shape[1], w_dn.shape[2]
    n_c = h_c * d_hc
    dn_alpha = (2 * depth) ** 0.25
    mla_scale = d_qk ** -0.5
    diff_scale = (d_hc // 2) ** -0.5
    assert d_nope == LANES and d_vh == LANES and d_hc == LANES and d_rope <= LANES
    assert past % CHUNK == 0 and sd <= CHUNK and n_exp <= LANES

    tp, td = nbp * sp, nbd * sd
    t_all = tp + td
    tm = _pick(t_all, MM_TM, BF16_SUBLANES)
    tm_p = _pick(tp, MM_TM, BF16_SUBLANES)
    tm_kd = _pick(nbd * skd, MM_TM, BF16_SUBLANES)
    halo = -(-(taps - 1) // 8) * 8
    n_assign = t_all * TOP_K
    n_blocks = (n_assign + n_exp * (MOE_BLK - 1) + MOE_BLK - 1) // MOE_BLK

    pos_p = jnp.arange(sp, dtype=I32)
    pos_d = past + jnp.arange(sd, dtype=I32)
    pos_all = jnp.concatenate([jnp.tile(pos_p, nbp), jnp.tile(pos_d, nbd)])
    cos_all, sin_all = _rope_tables(pos_all, d_rope)
    tq_p = _pick(sp, ATT_TQ, BF16_SUBLANES)
    if sp > tq_p:
        bias_p = _bias_table(rel_bias, pos_p, jnp.arange(tq_p, dtype=I32)).reshape(h_c, sp // tq_p, tq_p, tq_p)
    else:
        bias_p = _bias_table(rel_bias, pos_p, pos_p)[:, None]
    bias_d = _bias_table(rel_bias, pos_d, jnp.arange(skd, dtype=I32))[:, None]

    sizes = [c_conv, c_conv, q_lora, kv_lora, d_rope, n_c, n_c, n_c, 3 * d]
    offs = [0] + [int(v) for v in np.cumsum(sizes)]
    w_glu = w_in[:, :, offs[0]:offs[2]].astype(BF16)
    w_cq = w_in[:, :, offs[2]:offs[3]].astype(BF16)
    w_ckv = w_in[:, :, offs[3]:offs[4]].astype(BF16)
    w_kpe_f = w_in[:, :, offs[4]:offs[5]]
    w_kpe = _pad_cols(w_kpe_f, LANES).astype(BF16)
    w_kpe_rot = _pad_cols(_rot_half_cols(w_kpe_f), LANES).astype(BF16)
    w_rest = w_in[:, :, offs[5]:].astype(BF16)
    w_q1 = _pad_cols(w_uq, 2 * LANES).reshape(depth, q_lora, h_b * 2 * LANES).astype(BF16)
    w_q2 = _pad_cols(_rot_half_cols(w_uq[..., d_nope:]), LANES).reshape(depth, q_lora, h_b * LANES).astype(BF16)
    w_kv = w_ukv.reshape(depth, kv_lora, h_b * (d_nope + d_vh)).astype(BF16)
    w_a, w_b, w_c, w_o = (w.astype(BF16) for w in (w_a_out, w_b_out, w_c_out, w_out))
    w_r = _pad_cols(w_router, LANES).astype(BF16)
    b_r = jnp.concatenate([b_router, jnp.full((depth, LANES - n_exp), NEG_INF, F32)], axis=1).reshape(depth, 1, LANES)
    w_gu_b, w_dn_b = w_gu.astype(BF16), w_dn.astype(BF16)
    b_gu3, b_dn3 = b_gu.reshape(depth, n_exp, 1, 2 * d_exp), b_dn.reshape(depth, n_exp, 1, d)
    vec3 = lambda a: a.reshape(depth, 1, a.shape[-1])
    qn_g3, kvn_g3, b_dw3, cn_g3, cn_b3, sub_g3 = map(vec3, (qn_g, kvn_g, b_dw, cn_g, cn_b, sub_g))
    ln1_g3, ln1_b3, ln2_g3, ln2_b3 = map(vec3, (ln1_g, ln1_b, ln2_g, ln2_b))
    lam3 = tuple(map(vec3, (lam_q1, lam_k1, lam_q2, lam_k2)))
    hist_p = jnp.zeros((nbp, halo, c_conv), F32)
    hist_d_all = jnp.pad(cache_conv, ((0, 0), (0, 0), (halo - (taps - 1), 0), (0, 0)))
    ckv_cache_b = cache_mla_ckv.astype(BF16)
    kpe_cache_b = _pad_cols(cache_mla_kpe, LANES).astype(BF16)
    dk_cache_b = cache_diff_k.reshape(depth, nbd, past, n_c).astype(BF16)
    dv_cache_b = cache_diff_v.reshape(depth, nbd, past, n_c).astype(BF16)

    x = jnp.concatenate([x_prompt.reshape(tp, d), x_sample.reshape(td, d)], axis=0)
    xb = x.astype(BF16)
    tile_ij = lambda i, j: (i, j)
    tile_i0 = lambda i, j: (i, 0)
    states = [[] for _ in range(10)]

    for layer in range(depth):
        lam_init = 0.8 - 0.6 * math.exp(-0.3 * layer)
        lvec = lambda n: ((None, 1, n), lambda i, j, layer=layer: (layer, 0, 0))
        mm = functools.partial(fused_matmul, layer=layer)

        tn = _pick(c_conv, MM_TN, LANES)
        glu, = mm([xb], [(0, w_glu, tn, 0), (0, w_glu, tn, c_conv // tn)], [],
                  lambda a, e: [a[0] * jax.nn.sigmoid(a[1])], [(c_conv, tn, F32)],
                  m_rows=t_all, n_tiles=c_conv // tn, tm=tm, name="in_glu")
        hq, = mm([xb], [(0, w_cq, q_lora, 0)], [(qn_g3,) + lvec(q_lora)],
                 lambda a, e: [_rms(a[0], e[0])], [(q_lora, q_lora, BF16)],
                 m_rows=t_all, n_tiles=1, tm=tm, name="in_cq")
        ckv, ckv_b = mm([xb], [(0, w_ckv, kv_lora, 0)], [(kvn_g3,) + lvec(kv_lora)],
                        lambda a, e: [_rms(a[0], e[0])] * 2,
                        [(kv_lora, kv_lora, F32), (kv_lora, kv_lora, BF16)],
                        m_rows=t_all, n_tiles=1, tm=tm, name="in_ckv")
        kpe, kpe_b = mm([xb], [(0, w_kpe, LANES, 0), (0, w_kpe_rot, LANES, 0)],
                        [(cos_all, (tm, LANES), tile_i0), (sin_all, (tm, LANES), tile_i0)],
                        lambda a, e: [a[0] * e[0] + a[1] * e[1]] * 2, [(LANES, LANES, F32), (LANES, LANES, BF16)],
                        m_rows=t_all, n_tiles=1, tm=tm, name="in_kpe")
        tn = _pick(n_c, MM_TN, LANES)
        nq_t = n_c // tn

        def qkv_epilogue(a, e, nq_t=nq_t):
            j = pl.program_id(1)
            return [a[0], jnp.where(j < nq_t, a[0] * diff_scale, a[0])]

        dqkv, dqkv_b = mm([xb], [(0, w_rest, tn, 0)], [], qkv_epilogue,
                          [(3 * n_c, tn, F32), (3 * n_c, tn, BF16)],
                          m_rows=t_all, n_tiles=3 * nq_t, tm=tm, name="in_dqkv")
        tn = _pick(math.gcd(3 * d, 3 * n_c), MM_TN, LANES)
        gates, = mm([xb], [(0, w_rest, tn, 3 * n_c // tn)], [], lambda a, e: [jax.nn.sigmoid(a[0])],
                    [(3 * d, tn, BF16)], m_rows=t_all, n_tiles=3 * d // tn, tm=tm, name="in_gates")

        conv_args = (w_dw, b_dw3, cn_g3, cn_b3, layer)
        ca = jnp.concatenate([conv_module(glu, 0, nbp, sp, hist_p, *conv_args),
                              conv_module(glu, tp, nbd, sd, hist_d_all[layer], *conv_args)], axis=0)
        glu_p = glu[:tp].reshape(nbp, sp, c_conv)
        glu_d = glu[tp:].reshape(nbd, sd, c_conv)
        new_conv_p = jnp.concatenate([jnp.zeros((nbp, taps - 1, c_conv), F32), glu_p], axis=1)[:, -(taps - 1):]
        new_conv_d = jnp.concatenate([cache_conv[layer], glu_d], axis=1)[:, -(taps - 1):]

        def q_epilogue(a, e):
            rope = a[0][:, LANES:] * e[0] + a[1] * e[1]
            return [jnp.concatenate([a[0][:, :LANES], rope], axis=1) * mla_scale]

        q_mla, = mm([hq], [(0, w_q1, 2 * LANES, 0), (0, w_q2, LANES, 0)],
                    [(cos_all, (tm, LANES), tile_i0), (sin_all, (tm, LANES), tile_i0)],
                    q_epilogue, [(h_b * 2 * LANES, 2 * LANES, BF16)],
                    m_rows=t_all, n_tiles=h_b, tm=tm, name="mla_q")

        def kv_epilogue(a, e):
            return [jnp.concatenate([a[0][:, :LANES], e[0].astype(F32)], axis=1), a[0][:, LANES:]]

        kv_outs = [(h_b * 2 * LANES, 2 * LANES, BF16), (h_b * d_vh, d_vh, BF16)]
        k_p, v_p = mm([ckv_b], [(0, w_kv, 2 * LANES, 0)], [(kpe_b, (tm_p, LANES), tile_i0)],
                      kv_epilogue, kv_outs, m_rows=tp, n_tiles=h_b, tm=tm_p, name="mla_kv_prompt")
        ckv_hist = jnp.concatenate([ckv_cache_b[layer], ckv_b[tp:].reshape(nbd, sd, kv_lora)],
                                   axis=1).reshape(nbd * skd, kv_lora)
        kpe_hist = jnp.concatenate([kpe_cache_b[layer], kpe_b[tp:].reshape(nbd, sd, LANES)],
                                   axis=1).reshape(nbd * skd, LANES)
        k_d, v_d = mm([ckv_hist], [(0, w_kv, 2 * LANES, 0)], [(kpe_hist, (tm_kd, LANES), tile_i0)],
                      kv_epilogue, kv_outs, m_rows=nbd * skd, n_tiles=h_b, tm=tm_kd, name="mla_kv_decode")
        ob = jnp.concatenate([
            mla_attention(q_mla, k_p, v_p, nb=nbp, nh=h_b, sq=sp, sk=sp, q_off=0, q_row_off=0),
            mla_attention(q_mla, k_d, v_d, nb=nbd, nh=h_b, sq=sd, sk=skd, q_off=past, q_row_off=tp)], axis=0)

        dkv_new = dqkv_b[tp:, n_c:].reshape(nbd, sd, 2 * n_c)
        dk_hist = jnp.concatenate([dk_cache_b[layer], dkv_new[:, :, :n_c]], axis=1).reshape(nbd * skd, n_c)
        dv_hist = jnp.concatenate([dv_cache_b[layer], dkv_new[:, :, n_c:]], axis=1).reshape(nbd * skd, n_c)
        oc = jnp.concatenate([
            diff_attention(dqkv_b, dqkv_b, dqkv_b, bias_p, lam3, sub_g3, lam_init, layer,
                           nb=nbp, nh=h_c, sq=sp, sk=sp, q_off=0, q_row_off=0, q_col=0, k_col=h_c, v_col=2 * h_c),
            diff_attention(dqkv_b, dk_hist, dv_hist, bias_d, lam3, sub_g3, lam_init, layer,
                           nb=nbd, nh=h_c, sq=sd, sk=skd, q_off=past, q_row_off=tp, q_col=0, k_col=0, v_col=0)],
            axis=0)

        tn = _pick(d, MM_TN, LANES)
        nt = d // tn
        merged, = mm(
            [ca, ob, oc], [(0, w_a, tn, 0), (1, w_b, tn, 0), (2, w_c, tn, 0)],
            [(gates, (tm, tn), tile_ij), (gates, (tm, tn), lambda i, j, nt=nt: (i, j + nt)),
             (gates, (tm, tn), lambda i, j, nt=nt: (i, j + 2 * nt))],
            lambda a, e: [e[0].astype(F32) * a[0] + e[1].astype(F32) * a[1] + e[2].astype(F32) * a[2]],
            [(d, tn, BF16)], m_rows=t_all, n_tiles=nt, tm=tm, name="merge")
        t1, = mm([merged], [(0, w_o, tn, 0)], [(x, (tm, tn), tile_ij)],
                 lambda a, e: [dn_alpha * e[0] + a[0]], [(d, tn, F32)],
                 m_rows=t_all, n_tiles=nt, tm=tm, name="w_out")
        x1, x1b, x1p = layernorm_rows(t1, ln1_g3, ln1_b3, layer)

        top_i, top_g, top_rank, counts = router(x1b, w_r, b_r, layer, tm)
        dest, blk_exp, blk_valid, fill_info = _route_plan(top_i[:, :TOP_K], top_rank[:, :TOP_K], counts[0, :n_exp],
                                                          n_exp, MOE_BLK, n_blocks)
        xs = moe_dispatch(x1p, dest, fill_info, n_exp, MOE_BLK, n_blocks)
        y_rows = expert_ffn(xs, blk_exp, blk_valid, w_gu_b, b_gu3, w_dn_b, b_dn3, layer, MOE_BLK)
        x, xb = moe_combine_ln(y_rows, dest, top_g, x1, ln2_g3, ln2_b3, layer, dn_alpha)

        dqkv_p = dqkv[:tp]
        dqkv_d = dqkv[tp:]
        outs = (new_conv_p, ckv[:tp].reshape(nbp, sp, kv_lora), kpe[:tp, :d_rope].reshape(nbp, sp, d_rope),
                dqkv_p[:, n_c:2 * n_c].reshape(nbp, sp, h_c, d_hc), dqkv_p[:, 2 * n_c:].reshape(nbp, sp, h_c, d_hc),
                new_conv_d, ckv[tp:].reshape(nbd, sd, kv_lora), kpe[tp:, :d_rope].reshape(nbd, sd, d_rope),
                dqkv_d[:, n_c:2 * n_c].reshape(nbd, sd, h_c, d_hc), dqkv_d[:, 2 * n_c:].reshape(nbd, sd, h_c, d_hc))
        for buf, s in zip(states, outs):
            buf.append(s)

    y_prompt = x[:tp].reshape(nbp, sp, d)
    y_sample = x[tp:].reshape(nbd, sd, d)
    return (y_prompt, y_sample) + tuple(jnp.stack(b) for b in states)
```

```python
import functools
import math

import numpy as np
import jax
import jax.numpy as jnp
from jax import lax
from jax.experimental import pallas as pl
from jax.experimental.pallas import tpu as pltpu

BF16 = jnp.bfloat16
F32 = jnp.float32
U32 = jnp.uint32
I32 = jnp.int32

CHUNK = 64
ROPE_THETA = 10000.0
N_BUCKETS = 32
REL_MAX_DIST = 1024
TOP_K = 4
SWIGLU_LIMIT = 7.0
SWIGLU_ALPHA = 1.702
LN_EPS = 1e-5
NEG_INF = -1e30

VMEM_LIMIT_BYTES = 56 * 1024 * 1024
LANES = 128
BF16_SUBLANES = 16

MM_TM = 1280
MM_TN = 512
ATT_TQ = 256
ATT_TK = 256
ATT_SINGLE_PASS_SQ = 64
MLA_DECODE_HEADS = 4
DIFF_DECODE_HEADS = 8
CONV_TS = 256
CONV_ROWS = 32
LN_TM = 256
MOE_BLK = 256
DISPATCH_TC = 256
COMBINE_TC = 256
DMA_UNROLL = 8


def _pick(n, target, mult):
    best = None
    for d in range(mult, min(n, target) + 1, mult):
        if n % d == 0:
            best = d
    return n if best is None else best


def _rider_rows(n_rows, n_steps):
    for rows in range(BF16_SUBLANES, n_rows + 1, BF16_SUBLANES):
        if n_rows % rows == 0 and n_rows // rows <= n_steps:
            return rows
    return n_rows


def _params(*sem):
    return pltpu.CompilerParams(dimension_semantics=sem, vmem_limit_bytes=VMEM_LIMIT_BYTES)


def _fused_mm_kernel(*refs, pair_x, w_rows, n_x, n_w, n_e, n_r, epilogue, prologue):
    x_refs = refs[:n_x]
    w_refs = refs[n_x:n_x + n_w]
    e_refs = refs[n_x + n_w:n_x + n_w + n_e - n_r]
    o_refs = refs[n_x + n_w + n_e:]
    extras = [e[...] for e in e_refs]
    xs = [x[...] for x in x_refs]
    if prologue is not None:
        xs = prologue(xs, extras)
    accs = []
    for k, xi in enumerate(pair_x):
        if w_rows[k]:
            accs.append(lax.dot_general(xs[xi], w_refs[k][0], (((1,), (1,)), ((), ())),
                                        preferred_element_type=F32))
        else:
            accs.append(jnp.dot(xs[xi], w_refs[k][...], preferred_element_type=F32))
    outs = epilogue(accs, extras)
    for o_ref, o in zip(o_refs, outs):
        o_ref[...] = o.astype(o_ref.dtype)
    for src_ref, dst_ref in zip(refs[n_x + n_w + n_e - n_r:n_x + n_w + n_e], o_refs[len(outs):]):
        dst_ref[...] = src_ref[...].astype(dst_ref.dtype)


def fused_matmul(xs, pairs, extras, epilogue, outs, *, layer, m_rows, n_tiles, tm, prologue=None, riders=(),
                 name=None):
    assert m_rows % tm == 0
    grid = (m_rows // tm, n_tiles)
    in_specs = []
    for x in xs:
        in_specs.append(pl.BlockSpec((tm, x.shape[1]), lambda i, j: (i, 0)))
    w_rows = tuple(len(p) > 4 and p[4] for p in pairs)
    for p in pairs:
        w, wn, off = p[1], p[2], p[3]
        if len(p) > 4 and p[4]:
            assert off % BF16_SUBLANES == 0 and wn % BF16_SUBLANES == 0
            in_specs.append(pl.BlockSpec(
                (pl.Element(1), pl.Element(wn), pl.Element(w.shape[2])),
                lambda i, j, off=off, wn=wn: (layer, pl.multiple_of(off + j * wn, BF16_SUBLANES), 0)))
        else:
            in_specs.append(pl.BlockSpec((None, w.shape[1], wn), lambda i, j, off=off: (layer, 0, j + off)))
    for _, bshape, imap in extras:
        in_specs.append(pl.BlockSpec(bshape, imap))
    out_shape = [jax.ShapeDtypeStruct((m_rows, n), dt) for n, _, dt in outs]
    out_specs = [pl.BlockSpec((tm, bn), lambda i, j: (i, j)) for _, bn, _ in outs]
    for src, rows, n_rb, first in riders:
        assert src.shape[0] % rows == 0 and n_rb <= grid[0] * grid[1]
        step_block = lambda i, j, n_rb=n_rb: jnp.minimum(i * n_tiles + j, n_rb - 1)
        in_specs.append(pl.BlockSpec((rows, src.shape[1]), lambda i, j, f=first, sb=step_block: (f + sb(i, j), 0)))
        out_specs.append(pl.BlockSpec((rows, src.shape[1]), lambda i, j, sb=step_block: (sb(i, j), 0)))
        out_shape.append(jax.ShapeDtypeStruct((n_rb * rows, src.shape[1]), BF16))
    kern = functools.partial(_fused_mm_kernel, pair_x=tuple(p[0] for p in pairs), w_rows=w_rows, n_x=len(xs),
                             n_w=len(pairs), n_e=len(extras) + len(riders), n_r=len(riders),
                             epilogue=epilogue, prologue=prologue)
    semantics = ("arbitrary", "arbitrary") if riders else ("parallel", "arbitrary")
    return pl.pallas_call(
        kern, grid=grid, in_specs=in_specs, out_specs=out_specs, out_shape=out_shape,
        compiler_params=_params(*semantics), name=name,
    )(*xs, *[p[1] for p in pairs], *[e[0] for e in extras], *[r[0] for r in riders])


def _rms(x, g):
    return x * lax.rsqrt(jnp.mean(x * x, axis=-1, keepdims=True) + LN_EPS) * g


def _layernorm(x, g, b):
    mu = jnp.mean(x, axis=-1, keepdims=True)
    xc = x - mu
    var = jnp.mean(xc * xc, axis=-1, keepdims=True)
    return xc * lax.rsqrt(var + LN_EPS) * g + b


def _pack_bf16_pairs(x):
    h = x.shape[1] // 2
    lo = lax.bitcast_convert_type(x[:, :h].astype(BF16).astype(F32), U32)
    hi = lax.bitcast_convert_type(x[:, h:].astype(BF16).astype(F32), U32)
    return lax.shift_right_logical(lo, jnp.uint32(16)) | (hi & jnp.uint32(0xFFFF0000))


def _unpack_bf16_pairs(p):
    lo = lax.bitcast_convert_type(lax.shift_left(p, jnp.uint32(16)), F32).astype(BF16)
    hi = lax.bitcast_convert_type(p & jnp.uint32(0xFFFF0000), F32).astype(BF16)
    return lo, hi


def _ln_kernel(t_ref, g_ref, b_ref, o_ref, ob_ref, op_ref):
    y = _layernorm(t_ref[...], g_ref[...], b_ref[...])
    o_ref[...] = y
    ob_ref[...] = y.astype(BF16)
    op_ref[...] = _pack_bf16_pairs(y)


def layernorm_rows(t, g, b, layer):
    m, d = t.shape
    tm = _pick(m, LN_TM, BF16_SUBLANES)
    vec = pl.BlockSpec((None, 1, d), lambda i: (layer, 0, 0))
    row = lambda n: pl.BlockSpec((tm, n), lambda i: (i, 0))
    return pl.pallas_call(
        _ln_kernel, grid=(m // tm,),
        in_specs=[row(d), vec, vec],
        out_specs=[row(d), row(d), row(d // 2)],
        out_shape=[jax.ShapeDtypeStruct((m, d), F32), jax.ShapeDtypeStruct((m, d), BF16),
                   jax.ShapeDtypeStruct((m, d // 2), U32)],
        compiler_params=_params("parallel"), name="layernorm_rows",
    )(t, g, b)


def _conv_kernel(prev_ref, hist_ref, x_ref, w_ref, bdw_ref, g_ref, b_ref, o_ref, win_ref, *, ts, halo, taps, rows):
    @pl.when(pl.program_id(1) == 0)
    def _():
        win_ref[0:halo, :] = hist_ref[0]

    @pl.when(pl.program_id(1) != 0)
    def _():
        win_ref[0:halo, :] = prev_ref[...]

    win_ref[halo:halo + ts, :] = x_ref[...]
    lead = halo - (taps - 1)
    for r0 in range(0, ts, rows):
        acc = jnp.zeros((rows, x_ref.shape[1]), F32) + bdw_ref[...]
        for k in range(taps):
            acc = acc + w_ref[k:k + 1, :] * win_ref[r0 + lead + k:r0 + lead + k + rows, :]
        y = _layernorm(acc, g_ref[...], b_ref[...])
        o_ref[r0:r0 + rows, :] = (y * jax.nn.sigmoid(y)).astype(o_ref.dtype)


def conv_module(glu, row_off, nb, s, hist, w_dw, b_dw, cn_g, cn_b, layer):
    c = glu.shape[1]
    taps = w_dw.shape[1]
    halo = hist.shape[1]
    ts = _pick(s, CONV_TS, halo)
    rows = _pick(ts, CONV_ROWS, 8)
    nblk = s // ts
    assert row_off % ts == 0 and ts % halo == 0
    kern = functools.partial(_conv_kernel, ts=ts, halo=halo, taps=taps, rows=rows)
    off_main = row_off // ts
    off_prev = row_off // halo
    per = ts // halo
    vec = pl.BlockSpec((None, 1, c), lambda b, i: (layer, 0, 0))
    return pl.pallas_call(
        kern, grid=(nb, nblk),
        in_specs=[
            pl.BlockSpec((halo, c), lambda b, i: (off_prev + jnp.maximum((b * nblk + i) * per - 1, 0), 0)),
            pl.BlockSpec((1, halo, c), lambda b, i: (b, 0, 0)),
            pl.BlockSpec((ts, c), lambda b, i: (off_main + b * nblk + i, 0)),
            pl.BlockSpec((None, taps, c), lambda b, i: (layer, 0, 0)),
            vec, vec, vec,
        ],
        out_specs=pl.BlockSpec((ts, c), lambda b, i: (b * nblk + i, 0)),
        out_shape=jax.ShapeDtypeStruct((nb * s, c), BF16),
        scratch_shapes=[pltpu.VMEM((halo + ts, c), F32)],
        compiler_params=_params("parallel", "arbitrary"), name="conv_module",
    )(glu, hist, glu, w_dw, b_dw, cn_g, cn_b)


def _block_plan(sq, sk, q_off, tq, tk):
    plan = []
    for q0 in range(0, sq, tq):
        k_any = min(sk, ((q_off + q0 + tq - 1) // CHUNK + 1) * CHUNK)
        k_all = min(sk, ((q_off + q0) // CHUNK + 1) * CHUNK)
        blocks = []
        for k0 in range(0, k_any, tk):
            kl = min(tk, k_any - k0)
            blocks.append((k0, kl, k0 + kl > k_all))
        plan.append((q0, blocks))
    return plan


def _online_step(s, v, m, l, acc):
    m_new = jnp.maximum(m, jnp.max(s, axis=-1, keepdims=True))
    a = jnp.exp(m - m_new)
    p = jnp.exp(s - m_new)
    l = a * l + jnp.sum(p, axis=-1, keepdims=True)
    acc = a * acc + jnp.dot(p.astype(v.dtype), v, preferred_element_type=F32)
    return m_new, l, acc


def _qk(q, k):
    return lax.dot_general(q, k, (((1,), (1,)), ((), ())), preferred_element_type=F32)


def _att_tiles(sq, sk):
    tq = _pick(sq, ATT_TQ, BF16_SUBLANES)
    if sq <= ATT_SINGLE_PASS_SQ:
        return tq, sk
    tk = _pick(sk, ATT_TK, BF16_SUBLANES) if sk % ATT_TK else ATT_TK
    return tq, tk


def _head_group(nh, sq, target):
    return math.gcd(nh, target) if sq <= ATT_SINGLE_PASS_SQ else 1


def _mla_kernel(q_ref, k_ref, v_ref, o_ref, *, plan, tq, q_off, hg):
    dq = q_ref.shape[1] // hg
    dv = v_ref.shape[1] // hg
    for h in range(hg):
        for q0, blocks in plan:
            q = q_ref[q0:q0 + tq, h * dq:(h + 1) * dq]
            m = jnp.full((tq, 1), -jnp.inf, F32)
            l = jnp.zeros((tq, 1), F32)
            acc = jnp.zeros((tq, dv), F32)
            for k0, kl, masked in blocks:
                s = _qk(q, k_ref[k0:k0 + kl, h * dq:(h + 1) * dq])
                if masked:
                    qp = q_off + q0 + lax.broadcasted_iota(I32, (tq, kl), 0)
                    kp = k0 + lax.broadcasted_iota(I32, (tq, kl), 1)
                    s = jnp.where(kp // CHUNK <= qp // CHUNK, s, NEG_INF)
                m, l, acc = _online_step(s, v_ref[k0:k0 + kl, h * dv:(h + 1) * dv], m, l, acc)
            o_ref[q0:q0 + tq, h * dv:(h + 1) * dv] = (acc / l).astype(o_ref.dtype)


def mla_attention(q, k, v, *, nb, nh, sq, sk, q_off, q_row_off):
    dq = q.shape[1] // nh
    dv = v.shape[1] // nh
    tq, tk = _att_tiles(sq, sk)
    hg = _head_group(nh, sq, MLA_DECODE_HEADS)
    plan = _block_plan(sq, sk, q_off, tq, tk)
    assert q_row_off % sq == 0
    qb = q_row_off // sq
    kern = functools.partial(_mla_kernel, plan=plan, tq=tq, q_off=q_off, hg=hg)
    return pl.pallas_call(
        kern, grid=(nb, nh // hg),
        in_specs=[pl.BlockSpec((sq, hg * dq), lambda b, h: (qb + b, h)),
                  pl.BlockSpec((sk, hg * dq), lambda b, h: (b, h)),
                  pl.BlockSpec((sk, hg * dv), lambda b, h: (b, h))],
        out_specs=pl.BlockSpec((sq, hg * dv), lambda b, h: (b, h)),
        out_shape=jax.ShapeDtypeStruct((nb * sq, nh * dv), BF16),
        compiler_params=_params("parallel", "parallel"), name="mla_attention",
    )(q, k, v)


def _diff_kernel(q_ref, k_ref, v_ref, bias_ref, lq1_ref, lk1_ref, lq2_ref, lk2_ref, sg_ref, o_ref, *,
                 plan, tq, tk, lam_init, bias_blocked, hg):
    dh = LANES
    half = dh // 2
    lam = (jnp.exp(jnp.sum(lq1_ref[...] * lk1_ref[...], axis=-1, keepdims=True))
           - jnp.exp(jnp.sum(lq2_ref[...] * lk2_ref[...], axis=-1, keepdims=True)) + lam_init)
    lane = lax.broadcasted_iota(I32, (tq, dh), 1)
    for h in range(hg):
        cols = slice(h * dh, (h + 1) * dh)
        for q0, blocks in plan:
            q = q_ref[q0:q0 + tq, cols].astype(F32)
            q2 = jnp.concatenate([jnp.where(lane < half, q, 0.0), jnp.where(lane >= half, q, 0.0)],
                                 axis=0).astype(q_ref.dtype)
            m = jnp.full((2 * tq, 1), -jnp.inf, F32)
            l = jnp.zeros((2 * tq, 1), F32)
            acc = jnp.zeros((2 * tq, dh), F32)
            for k0, kl, _ in blocks:
                if bias_blocked:
                    bias = bias_ref[h, (q0 - k0) // tk]
                else:
                    bias = bias_ref[h, 0, :, k0:k0 + kl]
                s = _qk(q2, k_ref[k0:k0 + kl, cols]) + jnp.concatenate([bias, bias], axis=0)
                m, l, acc = _online_step(s, v_ref[k0:k0 + kl, cols], m, l, acc)
            o = acc / l
            o = o[:tq] - lam * o[tq:]
            o = _rms(o, sg_ref[...]) * (1.0 - lam_init)
            o_ref[q0:q0 + tq, cols] = o.astype(o_ref.dtype)


def diff_attention(q, k, v, bias, lam_vecs, sub_g, lam_init, layer, *, nb, nh, sq, sk, q_off, q_row_off,
                   q_col, k_col, v_col):
    dh = LANES
    tq, tk = _att_tiles(sq, sk)
    hg = _head_group(nh, sq, DIFF_DECODE_HEADS)
    assert q_col % hg == 0 and k_col % hg == 0 and v_col % hg == 0
    plan = _block_plan(sq, sk, q_off, tq, tk)
    bias_blocked = bias.shape[2:] == (tq, tk) and sq > tq
    qb = q_row_off // sq
    kern = functools.partial(_diff_kernel, plan=plan, tq=tq, tk=tk, lam_init=lam_init, bias_blocked=bias_blocked,
                             hg=hg)
    vec = lambda n: pl.BlockSpec((None, 1, n), lambda b, h: (layer, 0, 0))
    return pl.pallas_call(
        kern, grid=(nb, nh // hg),
        in_specs=[pl.BlockSpec((sq, hg * dh), lambda b, h: (qb + b, q_col // hg + h)),
                  pl.BlockSpec((sk, hg * dh), lambda b, h: (b, k_col // hg + h)),
                  pl.BlockSpec((sk, hg * dh), lambda b, h: (b, v_col // hg + h)),
                  pl.BlockSpec((hg,) + bias.shape[1:], lambda b, h: (h, 0, 0, 0)),
                  vec(dh // 2), vec(dh // 2), vec(dh // 2), vec(dh // 2), vec(dh)],
        out_specs=pl.BlockSpec((sq, hg * dh), lambda b, h: (b, h)),
        out_shape=jax.ShapeDtypeStruct((nb * sq, nh * dh), BF16),
        compiler_params=_params("parallel", "parallel"), name="diff_attention",
    )(q, k, v, bias, *lam_vecs, sub_g)


def _router_kernel(x_ref, w_ref, b_ref, tri_ref, idx_ref, gate_ref, rank_ref, cnt_ref, carry_ref):
    @pl.when(pl.program_id(0) == 0)
    def _():
        carry_ref[...] = jnp.zeros_like(carry_ref)

    logits = jnp.dot(x_ref[...], w_ref[...], preferred_element_type=F32) + b_ref[...]
    n = logits.shape[1]
    lane = lax.broadcasted_iota(I32, logits.shape, 1).astype(F32)
    vals, idxs = [], []
    for _ in range(TOP_K):
        mx = jnp.max(logits, axis=-1, keepdims=True)
        idx = jnp.min(jnp.where(logits == mx, lane, float(n)), axis=-1, keepdims=True)
        vals.append(mx)
        idxs.append(idx)
        logits = jnp.where(lane == idx, -jnp.inf, logits)
    es = [jnp.exp(v - vals[0]) for v in vals]
    den = es[0]
    for e in es[1:]:
        den = den + e
    sel = jnp.zeros(logits.shape, F32)
    for k in range(TOP_K):
        sel = jnp.where(lane == idxs[k], 1.0, sel)
    before = jnp.dot(tri_ref[...], sel.astype(BF16), preferred_element_type=F32) + carry_ref[0:1, :]
    idx_out = jnp.zeros(logits.shape, F32)
    gate_out = jnp.zeros(logits.shape, F32)
    rank_out = jnp.zeros(logits.shape, F32)
    for k in range(TOP_K):
        rank = jnp.sum(jnp.where(lane == idxs[k], before, 0.0), axis=-1, keepdims=True)
        idx_out = jnp.where(lane == float(k), idxs[k], idx_out)
        gate_out = jnp.where(lane == float(k), es[k] / den, gate_out)
        rank_out = jnp.where(lane == float(k), rank, rank_out)
    idx_ref[...] = idx_out.astype(I32)
    gate_ref[...] = gate_out
    rank_ref[...] = rank_out.astype(I32)
    carry_ref[0:1, :] = carry_ref[0:1, :] + jnp.sum(sel, axis=0, keepdims=True)
    cnt_ref[...] = jnp.broadcast_to(carry_ref[0:1, :], cnt_ref.shape).astype(I32)


def router(xb, w_r, b_r, layer, tm):
    m, d = xb.shape
    tri = (np.arange(tm)[:, None] > np.arange(tm)[None, :]).astype(np.float32)
    tri = jnp.asarray(tri, BF16)
    row = pl.BlockSpec((tm, LANES), lambda i: (i, 0))
    return pl.pallas_call(
        _router_kernel, grid=(m // tm,),
        in_specs=[pl.BlockSpec((tm, d), lambda i: (i, 0)),
                  pl.BlockSpec((None, d, LANES), lambda i: (layer, 0, 0)),
                  pl.BlockSpec((None, 1, LANES), lambda i: (layer, 0, 0)),
                  pl.BlockSpec((tm, tm), lambda i: (0, 0))],
        out_specs=[row, row, row, pl.BlockSpec((8, LANES), lambda i: (0, 0))],
        out_shape=[jax.ShapeDtypeStruct((m, LANES), I32), jax.ShapeDtypeStruct((m, LANES), F32),
                   jax.ShapeDtypeStruct((m, LANES), I32), jax.ShapeDtypeStruct((8, LANES), I32)],
        scratch_shapes=[pltpu.VMEM((8, LANES), F32)],
        compiler_params=_params("arbitrary"), name="router",
    )(xb, w_r, b_r, tri)


def _dispatch_kernel(info_ref, dest_ref, x_ref, o_ref, zero_ref, sem, fill_sem, *, n_exp, blk, n_blocks, tc):
    @pl.when(pl.program_id(0) == 0)
    def _():
        zero_ref[...] = jnp.zeros_like(zero_ref)

        def per_expert(e, c):
            first = info_ref[e]
            count = info_ref[n_exp + e]

            def start(r, c2):
                pltpu.make_async_copy(zero_ref.at[pl.ds(0, 1), :], o_ref.at[pl.ds(first + r, 1), :], fill_sem).start()
                return c2

            def wait(r, c2):
                pltpu.make_async_copy(zero_ref.at[pl.ds(0, 1), :], o_ref.at[pl.ds(first, 1), :], fill_sem).wait()
                return c2

            lax.fori_loop(0, count, start, 0)
            lax.fori_loop(0, count, wait, 0)
            return c

        lax.fori_loop(0, n_exp, per_expert, 0)
        used = info_ref[2 * n_exp]

        def tail_start(b, c):
            pltpu.make_async_copy(zero_ref, o_ref.at[pl.ds(b * blk, blk), :], fill_sem).start()
            return c

        def tail_wait(b, c):
            pltpu.make_async_copy(zero_ref, o_ref.at[pl.ds(b * blk, blk), :], fill_sem).wait()
            return c

        lax.fori_loop(used, n_blocks, tail_start, 0)
        lax.fori_loop(used, n_blocks, tail_wait, 0)

    def scatter(r, c):
        for k in range(TOP_K):
            pltpu.make_async_copy(x_ref.at[pl.ds(r, 1), :],
                                  o_ref.at[pl.ds(dest_ref[0, 0, k * tc + r], 1), :], sem).start(priority=k % 2)
        return c

    lax.fori_loop(0, tc, scatter, 0, unroll=DMA_UNROLL)
    for k in range(TOP_K):
        pltpu.make_async_copy(x_ref, o_ref.at[pl.ds(0, tc), :], sem).wait()


def moe_dispatch(xp, dest, fill_info, n_exp, blk, n_blocks):
    t, w = xp.shape
    tc = _pick(t, DISPATCH_TC, 8)
    assert blk % 8 == 0
    nsteps = t // tc
    dest_blocks = dest.reshape(nsteps, tc, TOP_K).transpose(0, 2, 1).reshape(nsteps, 1, TOP_K * tc)
    kern = functools.partial(_dispatch_kernel, n_exp=n_exp, blk=blk, n_blocks=n_blocks, tc=tc)
    return pl.pallas_call(
        kern,
        grid_spec=pltpu.PrefetchScalarGridSpec(
            num_scalar_prefetch=1, grid=(nsteps,),
            in_specs=[pl.BlockSpec((1, 1, TOP_K * tc), lambda i, info: (i, 0, 0), memory_space=pltpu.SMEM),
                      pl.BlockSpec((tc, w), lambda i, info: (i, 0))],
            out_specs=pl.BlockSpec(memory_space=pl.ANY),
            scratch_shapes=[pltpu.VMEM((blk, w), xp.dtype), pltpu.SemaphoreType.DMA(()),
                            pltpu.SemaphoreType.DMA(())]),
        out_shape=jax.ShapeDtypeStruct((n_blocks * blk, w), xp.dtype),
        compiler_params=_params("arbitrary"), name="moe_dispatch",
    )(fill_info, dest_blocks, xp)


def _expert_up_kernel(be_ref, valid_ref, x_ref, w_ref, b_ref, o_ref):
    de = o_ref.shape[1]
    half = w_ref.shape[0] // 2

    @pl.when(valid_ref[pl.program_id(0)] != 0)
    def _():
        lo, hi = _unpack_bf16_pairs(x_ref[...])
        gu = (jnp.dot(lo, w_ref[0:half, :], preferred_element_type=F32)
              + jnp.dot(hi, w_ref[half:, :], preferred_element_type=F32) + b_ref[...])
        g = jnp.minimum(gu[:, :de], SWIGLU_LIMIT)
        up = jnp.clip(gu[:, de:], -SWIGLU_LIMIT, SWIGLU_LIMIT)
        o_ref[...] = (g * jax.nn.sigmoid(SWIGLU_ALPHA * g) * (up + 1.0)).astype(o_ref.dtype)

    @pl.when(valid_ref[pl.program_id(0)] == 0)
    def _():
        o_ref[...] = jnp.zeros_like(o_ref)


def _expert_down_kernel(be_ref, valid_ref, a_ref, w_ref, b_ref, o_ref):
    @pl.when(valid_ref[pl.program_id(0)] != 0)
    def _():
        o_ref[...] = _pack_bf16_pairs(jnp.dot(a_ref[...], w_ref[...], preferred_element_type=F32) + b_ref[...])

    @pl.when(valid_ref[pl.program_id(0)] == 0)
    def _():
        o_ref[...] = jnp.zeros_like(o_ref)


def expert_ffn(xs, blk_exp, blk_valid, w_gu, b_gu, w_dn, b_dn, blk):
    p = xs.shape[0]
    d, n2 = w_gu.shape[1], w_gu.shape[2]
    de = n2 // 2
    nblk = p // blk
    act = pl.pallas_call(
        _expert_up_kernel,
        grid_spec=pltpu.PrefetchScalarGridSpec(
            num_scalar_prefetch=2, grid=(nblk,),
            in_specs=[pl.BlockSpec((blk, d // 2), lambda i, be, va: (i, 0)),
                      pl.BlockSpec((None, d, n2), lambda i, be, va: (be[i], 0, 0)),
                      pl.BlockSpec((None, 1, n2), lambda i, be, va: (be[i], 0, 0))],
            out_specs=pl.BlockSpec((blk, de), lambda i, be, va: (i, 0))),
        out_shape=jax.ShapeDtypeStruct((p, de), BF16),
        compiler_params=_params("arbitrary"), name="expert_up",
    )(blk_exp, blk_valid, xs, w_gu, b_gu)
    return pl.pallas_call(
        _expert_down_kernel,
        grid_spec=pltpu.PrefetchScalarGridSpec(
            num_scalar_prefetch=2, grid=(nblk,),
            in_specs=[pl.BlockSpec((blk, de), lambda i, be, va: (i, 0)),
                      pl.BlockSpec((None, de, d), lambda i, be, va: (be[i], 0, 0)),
                      pl.BlockSpec((None, 1, d), lambda i, be, va: (be[i], 0, 0))],
            out_specs=pl.BlockSpec((blk, d // 2), lambda i, be, va: (i, 0))),
        out_shape=jax.ShapeDtypeStruct((p, d // 2), U32),
        compiler_params=_params("arbitrary"), name="expert_down",
    )(blk_exp, blk_valid, act, w_dn, b_dn)


def _combine_kernel(dest_ref, dest_next_ref, y_ref, x_ref, gate_ref, g_ref, b_ref, o_ref, ob_ref, buf, sem, *,
                    alpha, tc):
    i = pl.program_id(0)
    n = TOP_K * tc

    def issue(d_ref, slot):
        def start(r2, c):
            for j in range(2):
                r = 2 * r2 + j
                pltpu.make_async_copy(y_ref.at[pl.ds(d_ref[0, 0, r], 1), :], buf.at[slot, pl.ds(r, 1), :],
                                      sem.at[slot]).start(priority=j)
            return c
        lax.fori_loop(0, n // 2, start, 0, unroll=DMA_UNROLL // 2)

    @pl.when(i == 0)
    def _():
        issue(dest_ref, 0)

    for slot in range(2):
        @pl.when(i % 2 == slot)
        def _(slot=slot):
            @pl.when(i + 1 < pl.num_programs(0))
            def _():
                issue(dest_next_ref, 1 - slot)

            pltpu.make_async_copy(y_ref.at[pl.ds(0, n), :], buf.at[slot], sem.at[slot]).wait()
            gates = gate_ref[...]
            y_lo = y_hi = None
            for k in range(TOP_K):
                p = buf[slot, k * tc:(k + 1) * tc, :]
                lo = gates[:, k:k + 1] * lax.bitcast_convert_type(lax.shift_left(p, jnp.uint32(16)), F32)
                hi = gates[:, k:k + 1] * lax.bitcast_convert_type(p & jnp.uint32(0xFFFF0000), F32)
                y_lo = lo if y_lo is None else y_lo + lo
                y_hi = hi if y_hi is None else y_hi + hi
            y = jnp.concatenate([y_lo, y_hi], axis=1)
            out = _layernorm(alpha * x_ref[...] + y, g_ref[...], b_ref[...])
            o_ref[...] = out
            ob_ref[...] = out.astype(BF16)


def moe_combine_ln(y_rows, dest, gates, x, g, b, layer, alpha):
    t, d = x.shape
    tc = _pick(t, COMBINE_TC, BF16_SUBLANES)
    nblk = t // tc
    dest_blocks = dest.reshape(nblk, tc, TOP_K).transpose(0, 2, 1).reshape(nblk, 1, TOP_K * tc)
    kern = functools.partial(_combine_kernel, alpha=alpha, tc=tc)
    vec = pl.BlockSpec((None, 1, d), lambda i: (layer, 0, 0))
    row = pl.BlockSpec((tc, d), lambda i: (i, 0))
    return pl.pallas_call(
        kern, grid=(nblk,),
        in_specs=[pl.BlockSpec((1, 1, TOP_K * tc), lambda i: (i, 0, 0), memory_space=pltpu.SMEM),
                  pl.BlockSpec((1, 1, TOP_K * tc), lambda i: (jnp.minimum(i + 1, nblk - 1), 0, 0),
                               memory_space=pltpu.SMEM),
                  pl.BlockSpec(memory_space=pl.ANY),
                  row, pl.BlockSpec((tc, LANES), lambda i: (i, 0)), vec, vec],
        out_specs=[row, row],
        out_shape=[jax.ShapeDtypeStruct((t, d), F32), jax.ShapeDtypeStruct((t, d), BF16)],
        scratch_shapes=[pltpu.VMEM((2, TOP_K * tc, d // 2), U32), pltpu.SemaphoreType.DMA((2,))],
        compiler_params=_params("arbitrary"), name="moe_combine_ln",
    )(dest_blocks, dest_blocks, y_rows, x, gates, g, b)


def _route_plan(idx, rank, counts, n_exp, blk, n_blocks):
    padded = (counts + blk - 1) // blk * blk
    pad_end = jnp.cumsum(padded)
    pad_start = pad_end - padded
    experts = jnp.arange(n_exp, dtype=I32)
    dest = rank + jnp.sum(jnp.where(idx[:, :, None] == experts, pad_start, 0), axis=-1)
    block_row0 = jnp.arange(n_blocks, dtype=I32) * blk
    blk_exp = jnp.minimum(jnp.sum((pad_end[None, :] <= block_row0[:, None]).astype(I32), axis=1), n_exp - 1)
    blk_valid = (block_row0 < pad_end[-1]).astype(I32)
    fill_info = jnp.concatenate([pad_start + counts, padded - counts, pad_end[-1:] // blk]).astype(I32)
    return dest.astype(I32), blk_exp.astype(I32), blk_valid, fill_info


def _rope_tables(pos, d_rope):
    half = d_rope // 2
    inv = 1.0 / (ROPE_THETA ** (jnp.arange(half, dtype=F32) * 2.0 / d_rope))
    ang = pos.astype(F32)[:, None] * inv[None, :]
    pad = jnp.zeros((pos.shape[0], LANES - d_rope), F32)
    cos = jnp.concatenate([jnp.cos(ang), jnp.cos(ang), pad], axis=1)
    sin = jnp.concatenate([jnp.sin(ang), jnp.sin(ang), pad], axis=1)
    return cos, sin


def _rel_bucket(rel):
    nb = N_BUCKETS // 2
    max_exact = nb // 2
    ret = jnp.where(rel > 0, nb, 0)
    n = jnp.abs(rel)
    nf = jnp.maximum(n, 1).astype(F32)
    large = max_exact + (jnp.log(nf / max_exact) / math.log(REL_MAX_DIST / max_exact) * (nb - max_exact)).astype(I32)
    large = jnp.minimum(large, nb - 1)
    return (ret + jnp.where(n < max_exact, n, large)).astype(I32)


def _bias_table(rel_bias, q_pos, k_pos):
    bucket = _rel_bucket(k_pos[None, :] - q_pos[:, None])
    vis = (k_pos[None, :] // CHUNK) <= (q_pos[:, None] // CHUNK)
    buckets = jnp.arange(N_BUCKETS, dtype=I32)
    bias = jnp.sum(jnp.where(bucket[None, :, :, None] == buckets, rel_bias.T[:, None, None, :], 0.0), axis=-1)
    return jnp.where(vis[None], bias, NEG_INF).astype(F32)


def _rot_half_cols(w):
    half = w.shape[-1] // 2
    return jnp.concatenate([-w[..., half:], w[..., :half]], axis=-1)


def _pad_cols(w, n):
    return jnp.pad(w, [(0, 0)] * (w.ndim - 1) + [(0, n - w.shape[-1])])


def kernel(x_prompt, x_sample, cache_conv, cache_mla_ckv, cache_mla_kpe, cache_diff_k, cache_diff_v, w_in, w_dw, b_dw, cn_g, cn_b, w_a_out, qn_g, w_uq, kvn_g, w_ukv, w_b_out, lam_q1, lam_k1, lam_q2, lam_k2, sub_g, w_c_out, rel_bias, w_out, ln1_g, ln1_b, w_router, b_router, w_gu, b_gu, w_dn, b_dn, ln2_g, ln2_b):
    nbp, sp, d = x_prompt.shape
    nbd, sd, _ = x_sample.shape
    depth = w_in.shape[0]
    past = cache_mla_ckv.shape[2]
    skd = past + sd
    taps, c_conv = w_dw.shape[1], w_dw.shape[2]
    q_lora, kv_lora = qn_g.shape[1], kvn_g.shape[1]
    h_b, d_qk = w_uq.shape[2], w_uq.shape[3]
    d_rope = cache_mla_kpe.shape[3]
    d_nope = d_qk - d_rope
    d_vh = w_ukv.shape[3] - d_nope
    h_c, d_hc = cache_diff_k.shape[3], cache_diff_k.shape[4]
    n_exp, d_exp = w_gu.shape[1], w_dn.shape[2]
    n_c = h_c * d_hc
    dn_alpha = (2 * depth) ** 0.25
    mla_scale = d_qk ** -0.5
    diff_scale = (d_hc // 2) ** -0.5
    assert d_nope == LANES and d_vh == LANES and d_hc == LANES and d_rope <= LANES
    assert past % CHUNK == 0 and sd <= CHUNK and n_exp <= LANES

    tp, td = nbp * sp, nbd * sd
    t_all = tp + td
    tm = _pick(t_all, MM_TM, BF16_SUBLANES)
    tm_p = _pick(tp, MM_TM, BF16_SUBLANES)
    tm_kd = _pick(nbd * skd, MM_TM, BF16_SUBLANES)
    halo = -(-(taps - 1) // 8) * 8
    n_assign = t_all * TOP_K
    n_blocks = (n_assign + n_exp * (MOE_BLK - 1) + MOE_BLK - 1) // MOE_BLK

    pos_p = jnp.arange(sp, dtype=I32)
    pos_d = past + jnp.arange(sd, dtype=I32)
    pos_all = jnp.concatenate([jnp.tile(pos_p, nbp), jnp.tile(pos_d, nbd)])
    cos_all, sin_all = _rope_tables(pos_all, d_rope)
    tq_p = _pick(sp, ATT_TQ, BF16_SUBLANES)
    if sp > tq_p:
        bias_p = _bias_table(rel_bias, pos_p, jnp.arange(tq_p, dtype=I32)).reshape(h_c, sp // tq_p, tq_p, tq_p)
    else:
        bias_p = _bias_table(rel_bias, pos_p, pos_p)[:, None]
    bias_d = _bias_table(rel_bias, pos_d, jnp.arange(skd, dtype=I32))[:, None]

    sizes = [c_conv, c_conv, q_lora, kv_lora, d_rope, n_c, n_c, n_c, 3 * d]
    offs = [0] + [int(v) for v in np.cumsum(sizes)]
    w_in_t = jnp.swapaxes(w_in, 1, 2).astype(BF16)
    w_kpe_t = w_in_t[:, offs[4]:offs[5], :]
    w_kpe_rot_t = jnp.pad(jnp.concatenate([-w_kpe_t[:, d_rope // 2:], w_kpe_t[:, :d_rope // 2]], axis=1),
                          ((0, 0), (0, LANES - d_rope), (0, 0)))
    w_q1 = _pad_cols(w_uq, 2 * LANES).reshape(depth, q_lora, h_b * 2 * LANES).astype(BF16)
    w_q2 = _pad_cols(_rot_half_cols(w_uq[..., d_nope:]), LANES).reshape(depth, q_lora, h_b * LANES).astype(BF16)
    w_kv = w_ukv.reshape(depth, kv_lora, h_b * (d_nope + d_vh)).astype(BF16)
    w_a, w_b, w_c, w_o = (w.astype(BF16) for w in (w_a_out, w_b_out, w_c_out, w_out))
    w_r = _pad_cols(w_router, LANES).astype(BF16)
    b_r = jnp.concatenate([b_router, jnp.full((depth, LANES - n_exp), NEG_INF, F32)], axis=1).reshape(depth, 1, LANES)
    b_gu3, b_dn3 = b_gu.reshape(depth, n_exp, 1, 2 * d_exp), b_dn.reshape(depth, n_exp, 1, d)
    w_gu2 = w_gu.reshape(depth * n_exp * d, 2 * d_exp)
    w_dn2 = w_dn.reshape(depth * n_exp * d_exp, d)
    vec3 = lambda a: a.reshape(depth, 1, a.shape[-1])
    qn_g3, kvn_g3, b_dw3, cn_g3, cn_b3, sub_g3 = map(vec3, (qn_g, kvn_g, b_dw, cn_g, cn_b, sub_g))
    ln1_g3, ln1_b3, ln2_g3, ln2_b3 = map(vec3, (ln1_g, ln1_b, ln2_g, ln2_b))
    lam3 = tuple(map(vec3, (lam_q1, lam_k1, lam_q2, lam_k2)))
    hist_p = jnp.zeros((nbp, halo, c_conv), F32)
    hist_d_all = jnp.pad(cache_conv, ((0, 0), (0, 0), (halo - (taps - 1), 0), (0, 0)))
    ckv_cache_b = cache_mla_ckv.astype(BF16)
    kpe_cache_b = _pad_cols(cache_mla_kpe, LANES).astype(BF16)
    dk_cache_b = cache_diff_k.reshape(depth, nbd, past, n_c).astype(BF16)
    dv_cache_b = cache_diff_v.reshape(depth, nbd, past, n_c).astype(BF16)

    x = jnp.concatenate([x_prompt.reshape(tp, d), x_sample.reshape(td, d)], axis=0)
    xb = x.astype(BF16)
    tile_ij = lambda i, j: (i, j)
    tile_i0 = lambda i, j: (i, 0)
    states = [[] for _ in range(10)]

    for layer in range(depth):
        lam_init = 0.8 - 0.6 * math.exp(-0.3 * layer)
        lvec = lambda n: ((None, 1, n), lambda i, j, layer=layer: (layer, 0, 0))
        mm = functools.partial(fused_matmul, layer=layer)

        tn = _pick(c_conv, MM_TN, LANES)
        glu, = mm([xb], [(0, w_in_t, tn, offs[0], True), (0, w_in_t, tn, offs[1], True)], [],
                  lambda a, e: [a[0] * jax.nn.sigmoid(a[1])], [(c_conv, tn, F32)],
                  m_rows=t_all, n_tiles=c_conv // tn, tm=tm, name="in_glu")
        hq, = mm([xb], [(0, w_in_t, q_lora, offs[2], True)], [(qn_g3,) + lvec(q_lora)],
                 lambda a, e: [_rms(a[0], e[0])], [(q_lora, q_lora, BF16)],
                 m_rows=t_all, n_tiles=1, tm=tm, name="in_cq")
        ckv, ckv_b = mm([xb], [(0, w_in_t, kv_lora, offs[3], True)], [(kvn_g3,) + lvec(kv_lora)],
                        lambda a, e: [_rms(a[0], e[0])] * 2,
                        [(kv_lora, kv_lora, F32), (kv_lora, kv_lora, BF16)],
                        m_rows=t_all, n_tiles=1, tm=tm, name="in_ckv")
        kpe, kpe_b = mm([xb], [(0, w_in_t, LANES, offs[4], True), (0, w_kpe_rot_t, LANES, 0, True)],
                        [(cos_all, (tm, LANES), tile_i0), (sin_all, (tm, LANES), tile_i0)],
                        lambda a, e: [a[0] * e[0] + a[1] * e[1]] * 2, [(LANES, LANES, F32), (LANES, LANES, BF16)],
                        m_rows=t_all, n_tiles=1, tm=tm, name="in_kpe")
        tn = _pick(n_c, MM_TN, LANES)
        nq_t = n_c // tn

        def qkv_epilogue(a, e, nq_t=nq_t):
            j = pl.program_id(1)
            return [a[0], jnp.where(j < nq_t, a[0] * diff_scale, a[0])]

        dqkv, dqkv_b = mm([xb], [(0, w_in_t, tn, offs[5], True)], [], qkv_epilogue,
                          [(3 * n_c, tn, F32), (3 * n_c, tn, BF16)],
                          m_rows=t_all, n_tiles=3 * nq_t, tm=tm, name="in_dqkv")
        tn = _pick(3 * d, MM_TN, LANES)
        gate_steps = (t_all // tm) * (3 * d // tn)
        riders = []
        for w_all in (w_gu2, w_dn2):
            per_layer = w_all.shape[0] // depth
            rows = _rider_rows(per_layer, gate_steps)
            riders.append((w_all, rows, per_layer // rows, layer * (per_layer // rows)))
        gates, w_gu_b, w_dn_b = mm([xb], [(0, w_in_t, tn, offs[8], True)], [], lambda a, e: [jax.nn.sigmoid(a[0])],
                                   [(3 * d, tn, BF16)], m_rows=t_all, n_tiles=3 * d // tn, tm=tm, riders=riders,
                                   name="in_gates")

        conv_args = (w_dw, b_dw3, cn_g3, cn_b3, layer)
        ca = jnp.concatenate([conv_module(glu, 0, nbp, sp, hist_p, *conv_args),
                              conv_module(glu, tp, nbd, sd, hist_d_all[layer], *conv_args)], axis=0)
        glu_p = glu[:tp].reshape(nbp, sp, c_conv)
        glu_d = glu[tp:].reshape(nbd, sd, c_conv)
        new_conv_p = jnp.concatenate([jnp.zeros((nbp, taps - 1, c_conv), F32), glu_p], axis=1)[:, -(taps - 1):]
        new_conv_d = jnp.concatenate([cache_conv[layer], glu_d], axis=1)[:, -(taps - 1):]

        def q_epilogue(a, e):
            rope = a[0][:, LANES:] * e[0] + a[1] * e[1]
            return [jnp.concatenate([a[0][:, :LANES], rope], axis=1) * mla_scale]

        q_mla, = mm([hq], [(0, w_q1, 2 * LANES, 0), (0, w_q2, LANES, 0)],
                    [(cos_all, (tm, LANES), tile_i0), (sin_all, (tm, LANES), tile_i0)],
                    q_epilogue, [(h_b * 2 * LANES, 2 * LANES, BF16)],
                    m_rows=t_all, n_tiles=h_b, tm=tm, name="mla_q")

        def kv_epilogue(a, e):
            return [jnp.concatenate([a[0][:, :LANES], e[0].astype(F32)], axis=1), a[0][:, LANES:]]

        kv_outs = [(h_b * 2 * LANES, 2 * LANES, BF16), (h_b * d_vh, d_vh, BF16)]
        k_p, v_p = mm([ckv_b], [(0, w_kv, 2 * LANES, 0)], [(kpe_b, (tm_p, LANES), tile_i0)],
                      kv_epilogue, kv_outs, m_rows=tp, n_tiles=h_b, tm=tm_p, name="mla_kv_prompt")
        ckv_hist = jnp.concatenate([ckv_cache_b[layer], ckv_b[tp:].reshape(nbd, sd, kv_lora)],
                                   axis=1).reshape(nbd * skd, kv_lora)
        kpe_hist = jnp.concatenate([kpe_cache_b[layer], kpe_b[tp:].reshape(nbd, sd, LANES)],
                                   axis=1).reshape(nbd * skd, LANES)
        k_d, v_d = mm([ckv_hist], [(0, w_kv, 2 * LANES, 0)], [(kpe_hist, (tm_kd, LANES), tile_i0)],
                      kv_epilogue, kv_outs, m_rows=nbd * skd, n_tiles=h_b, tm=tm_kd, name="mla_kv_decode")
        ob = jnp.concatenate([
            mla_attention(q_mla, k_p, v_p, nb=nbp, nh=h_b, sq=sp, sk=sp, q_off=0, q_row_off=0),
            mla_attention(q_mla, k_d, v_d, nb=nbd, nh=h_b, sq=sd, sk=skd, q_off=past, q_row_off=tp)], axis=0)

        dkv_new = dqkv_b[tp:, n_c:].reshape(nbd, sd, 2 * n_c)
        dk_hist = jnp.concatenate([dk_cache_b[layer], dkv_new[:, :, :n_c]], axis=1).reshape(nbd * skd, n_c)
        dv_hist = jnp.concatenate([dv_cache_b[layer], dkv_new[:, :, n_c:]], axis=1).reshape(nbd * skd, n_c)
        oc = jnp.concatenate([
            diff_attention(dqkv_b, dqkv_b, dqkv_b, bias_p, lam3, sub_g3, lam_init, layer,
                           nb=nbp, nh=h_c, sq=sp, sk=sp, q_off=0, q_row_off=0, q_col=0, k_col=h_c, v_col=2 * h_c),
            diff_attention(dqkv_b, dk_hist, dv_hist, bias_d, lam3, sub_g3, lam_init, layer,
                           nb=nbd, nh=h_c, sq=sd, sk=skd, q_off=past, q_row_off=tp, q_col=0, k_col=0, v_col=0)],
            axis=0)

        tn = _pick(d, MM_TN, LANES)
        nt = d // tn
        merged, = mm(
            [ca, ob, oc], [(0, w_a, tn, 0), (1, w_b, tn, 0), (2, w_c, tn, 0)],
            [(gates, (tm, tn), tile_ij), (gates, (tm, tn), lambda i, j, nt=nt: (i, j + nt)),
             (gates, (tm, tn), lambda i, j, nt=nt: (i, j + 2 * nt))],
            lambda a, e: [e[0].astype(F32) * a[0] + e[1].astype(F32) * a[1] + e[2].astype(F32) * a[2]],
            [(d, tn, BF16)], m_rows=t_all, n_tiles=nt, tm=tm, name="merge")
        t1, = mm([merged], [(0, w_o, tn, 0)], [(x, (tm, tn), tile_ij)],
                 lambda a, e: [dn_alpha * e[0] + a[0]], [(d, tn, F32)],
                 m_rows=t_all, n_tiles=nt, tm=tm, name="w_out")
        x1, x1b, x1p = layernorm_rows(t1, ln1_g3, ln1_b3, layer)

        top_i, top_g, top_rank, counts = router(x1b, w_r, b_r, layer, tm)
        dest, blk_exp, blk_valid, fill_info = _route_plan(
            top_i[:, :TOP_K], top_rank[:, :TOP_K], counts[0, :n_exp], n_exp, MOE_BLK, n_blocks)
        xs = moe_dispatch(x1p, dest, fill_info, n_exp, MOE_BLK, n_blocks)
        y_rows = expert_ffn(xs, blk_exp, blk_valid, w_gu_b.reshape(n_exp, d, 2 * d_exp), b_gu3[layer],
                            w_dn_b.reshape(n_exp, d_exp, d), b_dn3[layer], MOE_BLK)
        x, xb = moe_combine_ln(y_rows, dest, top_g, x1, ln2_g3, ln2_b3, layer, dn_alpha)

        dqkv_p = dqkv[:tp]
        dqkv_d = dqkv[tp:]
        outs = (new_conv_p, ckv[:tp].reshape(nbp, sp, kv_lora), kpe[:tp, :d_rope].reshape(nbp, sp, d_rope),
                dqkv_p[:, n_c:2 * n_c].reshape(nbp, sp, h_c, d_hc), dqkv_p[:, 2 * n_c:].reshape(nbp, sp, h_c, d_hc),
                new_conv_d, ckv[tp:].reshape(nbd, sd, kv_lora), kpe[tp:, :d_rope].reshape(nbd, sd, d_rope),
                dqkv_d[:, n_c:2 * n_c].reshape(nbd, sd, h_c, d_hc), dqkv_d[:, 2 * n_c:].reshape(nbd, sd, h_c, d_hc))
        for buf, s in zip(states, outs):
            buf.append(s)

    y_prompt = x[:tp].reshape(nbp, sp, d)
    y_sample = x[tp:].reshape(nbd, sd, d)
    return (y_prompt, y_sample) + tuple(jnp.stack(b) for b in states)
```

```python
import functools
import math

import numpy as np
import jax
import jax.numpy as jnp
from jax import lax
from jax.experimental import pallas as pl
from jax.experimental.pallas import tpu as pltpu

BF16 = jnp.bfloat16
F32 = jnp.float32
U32 = jnp.uint32
I32 = jnp.int32

CHUNK = 64
ROPE_THETA = 10000.0
N_BUCKETS = 32
REL_MAX_DIST = 1024
TOP_K = 4
SWIGLU_LIMIT = 7.0
SWIGLU_ALPHA = 1.702
LN_EPS = 1e-5
NEG_INF = -1e30

VMEM_LIMIT_BYTES = 56 * 1024 * 1024
LANES = 128
BF16_SUBLANES = 16

MM_TM = 1280
MM_TN = 512
ATT_TQ = 256
ATT_TK = 256
ATT_SINGLE_PASS_SQ = 64
MLA_DECODE_HEADS = 4
DIFF_DECODE_HEADS = 8
CONV_TS = 256
CONV_ROWS = 32
LN_TM = 256
MOE_BLK = 256
DISPATCH_TC = 256
COMBINE_TC = 256
DMA_UNROLL = 8


def _pick(n, target, mult):
    best = None
    for d in range(mult, min(n, target) + 1, mult):
        if n % d == 0:
            best = d
    return n if best is None else best


def _rider_rows(n_rows, n_steps):
    for rows in range(BF16_SUBLANES, n_rows + 1, BF16_SUBLANES):
        if n_rows % rows == 0 and n_rows // rows <= n_steps:
            return rows
    return n_rows


def _params(*sem):
    return pltpu.CompilerParams(dimension_semantics=sem, vmem_limit_bytes=VMEM_LIMIT_BYTES)


def _fused_mm_kernel(*refs, pair_x, w_rows, n_x, n_w, n_e, n_r, n_prev, placed, epilogue, prologue):
    x_refs = refs[:n_x]
    w_refs = refs[n_x:n_x + n_w]
    e_refs = refs[n_x + n_w:n_x + n_w + n_e - n_r]
    o_refs = refs[n_x + n_w + n_e + n_prev:]
    extras = [e[...] for e in e_refs]
    xs = [x[...] for x in x_refs]
    if prologue is not None:
        xs = prologue(xs, extras)
    accs = []
    for k, xi in enumerate(pair_x):
        if w_rows[k]:
            accs.append(lax.dot_general(xs[xi], w_refs[k][0], (((1,), (1,)), ((), ())),
                                        preferred_element_type=F32))
        else:
            accs.append(jnp.dot(xs[xi], w_refs[k][...], preferred_element_type=F32))
    outs = epilogue(accs, extras)
    for o_ref, o in zip(o_refs, outs):
        o_ref[...] = o.astype(o_ref.dtype)
    for src_ref, dst_ref in zip(refs[n_x + n_w + n_e - n_r:n_x + n_w + n_e], o_refs[len(outs):]):
        dst_ref[...] = src_ref[...].astype(dst_ref.dtype)
    i, j = pl.program_id(0), pl.program_id(1)
    for dst_ref, (when, pick) in zip(o_refs[len(outs) + n_r:], placed):
        @pl.when(when(i, j))
        def _(dst_ref=dst_ref, pick=pick):
            dst_ref[...] = pick(accs, extras).astype(dst_ref.dtype)


def fused_matmul(xs, pairs, extras, epilogue, outs, *, layer, m_rows, n_tiles, tm, prologue=None, riders=(),
                 placed=(), name=None):
    assert m_rows % tm == 0
    grid = (m_rows // tm, n_tiles)
    in_specs = []
    for x in xs:
        in_specs.append(pl.BlockSpec((tm, x.shape[1]), lambda i, j: (i, 0)))
    w_rows = tuple(len(p) > 4 and p[4] for p in pairs)
    for p in pairs:
        w, wn, off = p[1], p[2], p[3]
        if len(p) > 4 and p[4]:
            assert off % BF16_SUBLANES == 0 and wn % BF16_SUBLANES == 0
            in_specs.append(pl.BlockSpec(
                (pl.Element(1), pl.Element(wn), pl.Element(w.shape[2])),
                lambda i, j, off=off, wn=wn: (layer, pl.multiple_of(off + j * wn, BF16_SUBLANES), 0)))
        else:
            in_specs.append(pl.BlockSpec((None, w.shape[1], wn), lambda i, j, off=off: (layer, 0, j + off)))
    for _, bshape, imap in extras:
        in_specs.append(pl.BlockSpec(bshape, imap))
    out_shape = [jax.ShapeDtypeStruct((m_rows, n), dt) for n, _, dt in outs]
    out_specs = [pl.BlockSpec((tm, bn), lambda i, j: (i, j)) for _, bn, _ in outs]
    for src, rows, n_rb, first in riders:
        assert src.shape[0] % rows == 0 and n_rb <= grid[0] * grid[1]
        step_block = lambda i, j, n_rb=n_rb: jnp.minimum(i * n_tiles + j, n_rb - 1)
        in_specs.append(pl.BlockSpec((rows, src.shape[1]), lambda i, j, f=first, sb=step_block: (f + sb(i, j), 0)))
        out_specs.append(pl.BlockSpec((rows, src.shape[1]), lambda i, j, sb=step_block: (sb(i, j), 0)))
        out_shape.append(jax.ShapeDtypeStruct((n_rb * rows, src.shape[1]), BF16))
    inputs = [*xs, *[p[1] for p in pairs], *[e[0] for e in extras], *[r[0] for r in riders]]
    aliases = {}
    for shape, dtype, block, imap, _, _, prev in placed:
        out_shape.append(jax.ShapeDtypeStruct(shape, dtype))
        out_specs.append(pl.BlockSpec(block, imap))
        if prev is not None:
            aliases[len(inputs)] = len(out_shape) - 1
            in_specs.append(pl.BlockSpec(memory_space=pl.ANY))
            inputs.append(prev)
    kern = functools.partial(_fused_mm_kernel, pair_x=tuple(p[0] for p in pairs), w_rows=w_rows, n_x=len(xs),
                             n_w=len(pairs), n_e=len(extras) + len(riders), n_r=len(riders), n_prev=len(aliases),
                             placed=tuple((p[4], p[5]) for p in placed), epilogue=epilogue, prologue=prologue)
    semantics = ("arbitrary", "arbitrary") if riders or placed else ("parallel", "arbitrary")
    return pl.pallas_call(
        kern, grid=grid, in_specs=in_specs, out_specs=out_specs, out_shape=out_shape,
        input_output_aliases=aliases, compiler_params=_params(*semantics), name=name,
    )(*inputs)


def _rms(x, g):
    return x * lax.rsqrt(jnp.mean(x * x, axis=-1, keepdims=True) + LN_EPS) * g


def _layernorm(x, g, b):
    mu = jnp.mean(x, axis=-1, keepdims=True)
    xc = x - mu
    var = jnp.mean(xc * xc, axis=-1, keepdims=True)
    return xc * lax.rsqrt(var + LN_EPS) * g + b


def _pack_bf16_pairs(x):
    h = x.shape[1] // 2
    lo = lax.bitcast_convert_type(x[:, :h].astype(BF16).astype(F32), U32)
    hi = lax.bitcast_convert_type(x[:, h:].astype(BF16).astype(F32), U32)
    return lax.shift_right_logical(lo, jnp.uint32(16)) | (hi & jnp.uint32(0xFFFF0000))


def _unpack_bf16_pairs(p):
    lo = lax.bitcast_convert_type(lax.shift_left(p, jnp.uint32(16)), F32).astype(BF16)
    hi = lax.bitcast_convert_type(p & jnp.uint32(0xFFFF0000), F32).astype(BF16)
    return lo, hi


def _ln_kernel(t_ref, g_ref, b_ref, o_ref, ob_ref, op_ref):
    y = _layernorm(t_ref[...], g_ref[...], b_ref[...])
    o_ref[...] = y
    ob_ref[...] = y.astype(BF16)
    op_ref[...] = _pack_bf16_pairs(y)


def layernorm_rows(t, g, b, layer):
    m, d = t.shape
    tm = _pick(m, LN_TM, BF16_SUBLANES)
    vec = pl.BlockSpec((None, 1, d), lambda i: (layer, 0, 0))
    row = lambda n: pl.BlockSpec((tm, n), lambda i: (i, 0))
    return pl.pallas_call(
        _ln_kernel, grid=(m // tm,),
        in_specs=[row(d), vec, vec],
        out_specs=[row(d), row(d), row(d // 2)],
        out_shape=[jax.ShapeDtypeStruct((m, d), F32), jax.ShapeDtypeStruct((m, d), BF16),
                   jax.ShapeDtypeStruct((m, d // 2), U32)],
        compiler_params=_params("parallel"), name="layernorm_rows",
    )(t, g, b)


def _conv_kernel(prev_ref, hist_ref, x_ref, w_ref, bdw_ref, g_ref, b_ref, o_ref, win_ref, *, ts, halo, taps, rows):
    def fill(halo_rows):
        x = x_ref[...]
        for o in range(8):
            win_ref[o, 8 - o:8 - o + halo, :] = halo_rows
            win_ref[o, 8 - o + halo:8 - o + halo + ts, :] = x

    @pl.when(pl.program_id(1) == 0)
    def _():
        fill(hist_ref[0])

    @pl.when(pl.program_id(1) != 0)
    def _():
        fill(prev_ref[...])

    lead = halo - (taps - 1)
    for r0 in range(0, ts, rows):
        acc = jnp.zeros((rows, x_ref.shape[1]), F32) + bdw_ref[...]
        for k in range(taps):
            o = (lead + k) % 8
            start = 8 + r0 + lead + k - o
            acc = acc + w_ref[k:k + 1, :] * win_ref[o, start:start + rows, :]
        y = _layernorm(acc, g_ref[...], b_ref[...])
        o_ref[r0:r0 + rows, :] = (y * jax.nn.sigmoid(y)).astype(o_ref.dtype)


def _placed(kern, in_specs, inputs, into):
    if into is None:
        return kern, in_specs, inputs, {}
    n_in = len(inputs)
    drop = lambda *refs: kern(*refs[:n_in], *refs[n_in + 1:])
    return (drop, in_specs + [pl.BlockSpec(memory_space=pl.ANY)], inputs + [into],
            {"input_output_aliases": {n_in: 0}})


def conv_module(glu, row_off, nb, s, hist, w_dw, b_dw, cn_g, cn_b, layer, total_rows, into=None):
    c = glu.shape[1]
    taps = w_dw.shape[1]
    halo = hist.shape[1]
    ts = _pick(s, CONV_TS, halo)
    rows = _pick(ts, CONV_ROWS, 8)
    nblk = s // ts
    assert row_off % ts == 0 and ts % halo == 0
    kern = functools.partial(_conv_kernel, ts=ts, halo=halo, taps=taps, rows=rows)
    off_main = row_off // ts
    off_prev = row_off // halo
    per = ts // halo
    vec = pl.BlockSpec((None, 1, c), lambda b, i: (layer, 0, 0))
    in_specs = [
        pl.BlockSpec((halo, c), lambda b, i: (off_prev + jnp.maximum((b * nblk + i) * per - 1, 0), 0)),
        pl.BlockSpec((1, halo, c), lambda b, i: (b, 0, 0)),
        pl.BlockSpec((ts, c), lambda b, i: (off_main + b * nblk + i, 0)),
        pl.BlockSpec((None, taps, c), lambda b, i: (layer, 0, 0)),
        vec, vec, vec,
    ]
    kern, in_specs, inputs, alias = _placed(kern, in_specs, [glu, hist, glu, w_dw, b_dw, cn_g, cn_b], into)
    return pl.pallas_call(
        kern, grid=(nb, nblk), in_specs=in_specs,
        out_specs=pl.BlockSpec((ts, c), lambda b, i: (off_main + b * nblk + i, 0)),
        out_shape=jax.ShapeDtypeStruct((total_rows, c), BF16),
        scratch_shapes=[pltpu.VMEM((8, 8 + halo + ts, c), F32)],
        compiler_params=_params("parallel", "arbitrary"), name="conv_module", **alias,
    )(*inputs)


def _block_plan(sq, sk, q_off, tq, tk):
    plan = []
    for q0 in range(0, sq, tq):
        k_any = min(sk, ((q_off + q0 + tq - 1) // CHUNK + 1) * CHUNK)
        k_all = min(sk, ((q_off + q0) // CHUNK + 1) * CHUNK)
        blocks = []
        for k0 in range(0, k_any, tk):
            kl = min(tk, k_any - k0)
            blocks.append((k0, kl, k0 + kl > k_all))
        plan.append((q0, blocks))
    return plan


def _online_step(s, v, m, l, acc):
    m_new = jnp.maximum(m, jnp.max(s, axis=-1, keepdims=True))
    a = jnp.exp(m - m_new)
    p = jnp.exp(s - m_new)
    l = a * l + jnp.sum(p, axis=-1, keepdims=True)
    acc = a * acc + jnp.dot(p.astype(v.dtype), v, preferred_element_type=F32)
    return m_new, l, acc


def _qk(q, k):
    return lax.dot_general(q, k, (((1,), (1,)), ((), ())), preferred_element_type=F32)


def _att_tiles(sq, sk, tq_target=None):
    tq = _pick(sq, tq_target or ATT_TK, BF16_SUBLANES)
    if sq <= ATT_SINGLE_PASS_SQ:
        return tq, sk
    tk = _pick(sk, ATT_TK, BF16_SUBLANES) if sk % ATT_TK else ATT_TK
    return tq, tk


def _head_group(nh, sq, target):
    return math.gcd(nh, target) if sq <= ATT_SINGLE_PASS_SQ else 1


def _mla_kernel(q_ref, k_ref, v_ref, o_ref, *, plan, tq, q_off, hg):
    dq = q_ref.shape[1] // hg
    dv = v_ref.shape[1] // hg
    for h in range(hg):
        for q0, blocks in plan:
            q = q_ref[q0:q0 + tq, h * dq:(h + 1) * dq]
            m = jnp.full((tq, 1), -jnp.inf, F32)
            l = jnp.zeros((tq, 1), F32)
            acc = jnp.zeros((tq, dv), F32)
            for k0, kl, masked in blocks:
                s = _qk(q, k_ref[k0:k0 + kl, h * dq:(h + 1) * dq])
                if masked:
                    qp = q_off + q0 + lax.broadcasted_iota(I32, (tq, kl), 0)
                    kp = k0 + lax.broadcasted_iota(I32, (tq, kl), 1)
                    s = jnp.where(kp // CHUNK <= qp // CHUNK, s, NEG_INF)
                m, l, acc = _online_step(s, v_ref[k0:k0 + kl, h * dv:(h + 1) * dv], m, l, acc)
            o_ref[q0:q0 + tq, h * dv:(h + 1) * dv] = (acc / l).astype(o_ref.dtype)


def mla_attention(q, k, v, *, nb, nh, sq, sk, q_off, q_row_off, total_rows, into=None):
    dq = q.shape[1] // nh
    dv = v.shape[1] // nh
    tq, tk = _att_tiles(sq, sk, ATT_TQ)
    hg = _head_group(nh, sq, MLA_DECODE_HEADS)
    plan = _block_plan(sq, sk, q_off, tq, tk)
    assert q_row_off % sq == 0
    qb = q_row_off // sq
    kern = functools.partial(_mla_kernel, plan=plan, tq=tq, q_off=q_off, hg=hg)
    in_specs = [pl.BlockSpec((sq, hg * dq), lambda b, h: (qb + b, h)),
                pl.BlockSpec((sk, hg * dq), lambda b, h: (b, h)),
                pl.BlockSpec((sk, hg * dv), lambda b, h: (b, h))]
    kern, in_specs, inputs, alias = _placed(kern, in_specs, [q, k, v], into)
    return pl.pallas_call(
        kern, grid=(nb, nh // hg), in_specs=in_specs,
        out_specs=pl.BlockSpec((sq, hg * dv), lambda b, h: (qb + b, h)),
        out_shape=jax.ShapeDtypeStruct((total_rows, nh * dv), BF16),
        compiler_params=_params("parallel", "parallel"), name="mla_attention", **alias,
    )(*inputs)


def _diff_kernel(q_ref, k_ref, v_ref, bias_ref, lq1_ref, lk1_ref, lq2_ref, lk2_ref, sg_ref, o_ref, *,
                 plan, tq, tk, lam_init, bias_blocked, hg):
    dh = LANES
    half = dh // 2
    lam = (jnp.exp(jnp.sum(lq1_ref[...] * lk1_ref[...], axis=-1, keepdims=True))
           - jnp.exp(jnp.sum(lq2_ref[...] * lk2_ref[...], axis=-1, keepdims=True)) + lam_init)
    lane = lax.broadcasted_iota(I32, (tq, dh), 1)
    for h in range(hg):
        cols = slice(h * dh, (h + 1) * dh)
        for q0, blocks in plan:
            q = q_ref[q0:q0 + tq, cols].astype(F32)
            q2 = jnp.concatenate([jnp.where(lane < half, q, 0.0), jnp.where(lane >= half, q, 0.0)],
                                 axis=0).astype(q_ref.dtype)
            m = jnp.full((2 * tq, 1), -jnp.inf, F32)
            l = jnp.zeros((2 * tq, 1), F32)
            acc = jnp.zeros((2 * tq, dh), F32)
            for k0, kl, _ in blocks:
                if bias_blocked:
                    bias = bias_ref[h, (q0 - k0) // tk]
                else:
                    bias = bias_ref[h, 0, :, k0:k0 + kl]
                s = _qk(q2, k_ref[k0:k0 + kl, cols]) + jnp.concatenate([bias, bias], axis=0)
                m, l, acc = _online_step(s, v_ref[k0:k0 + kl, cols], m, l, acc)
            o = acc / l
            o = o[:tq] - lam * o[tq:]
            o = _rms(o, sg_ref[...]) * (1.0 - lam_init)
            o_ref[q0:q0 + tq, cols] = o.astype(o_ref.dtype)


def diff_attention(q, k, v, bias, lam_vecs, sub_g, lam_init, layer, *, nb, nh, sq, sk, q_off, q_row_off,
                   q_col, k_col, v_col, total_rows, into=None):
    dh = LANES
    tq, tk = _att_tiles(sq, sk)
    hg = _head_group(nh, sq, DIFF_DECODE_HEADS)
    assert q_col % hg == 0 and k_col % hg == 0 and v_col % hg == 0
    plan = _block_plan(sq, sk, q_off, tq, tk)
    bias_blocked = bias.shape[2:] == (tq, tk) and sq > tq
    qb = q_row_off // sq
    kern = functools.partial(_diff_kernel, plan=plan, tq=tq, tk=tk, lam_init=lam_init, bias_blocked=bias_blocked,
                             hg=hg)
    vec = lambda n: pl.BlockSpec((None, 1, n), lambda b, h: (layer, 0, 0))
    in_specs = [pl.BlockSpec((sq, hg * dh), lambda b, h: (qb + b, q_col // hg + h)),
                pl.BlockSpec((sk, hg * dh), lambda b, h: (b, k_col // hg + h)),
                pl.BlockSpec((sk, hg * dh), lambda b, h: (b, v_col // hg + h)),
                pl.BlockSpec((hg,) + bias.shape[1:], lambda b, h: (h, 0, 0, 0)),
                vec(dh // 2), vec(dh // 2), vec(dh // 2), vec(dh // 2), vec(dh)]
    kern, in_specs, inputs, alias = _placed(kern, in_specs, [q, k, v, bias, *lam_vecs, sub_g], into)
    return pl.pallas_call(
        kern, grid=(nb, nh // hg), in_specs=in_specs,
        out_specs=pl.BlockSpec((sq, hg * dh), lambda b, h: (qb + b, h)),
        out_shape=jax.ShapeDtypeStruct((total_rows, nh * dh), BF16),
        compiler_params=_params("parallel", "parallel"), name="diff_attention", **alias,
    )(*inputs)


def _router_kernel(x_ref, w_ref, b_ref, tri_ref, idx_ref, gate_ref, rank_ref, cnt_ref, carry_ref):
    @pl.when(pl.program_id(0) == 0)
    def _():
        carry_ref[...] = jnp.zeros_like(carry_ref)

    logits = jnp.dot(x_ref[...], w_ref[...], preferred_element_type=F32) + b_ref[...]
    n = logits.shape[1]
    lane = lax.broadcasted_iota(I32, logits.shape, 1).astype(F32)
    vals, idxs = [], []
    for _ in range(TOP_K):
        mx = jnp.max(logits, axis=-1, keepdims=True)
        idx = jnp.min(jnp.where(logits == mx, lane, float(n)), axis=-1, keepdims=True)
        vals.append(mx)
        idxs.append(idx)
        logits = jnp.where(lane == idx, -jnp.inf, logits)
    es = [jnp.exp(v - vals[0]) for v in vals]
    den = es[0]
    for e in es[1:]:
        den = den + e
    sel = jnp.zeros(logits.shape, F32)
    for k in range(TOP_K):
        sel = jnp.where(lane == idxs[k], 1.0, sel)
    before = jnp.dot(tri_ref[...], sel.astype(BF16), preferred_element_type=F32) + carry_ref[0:1, :]
    idx_out = jnp.zeros(logits.shape, F32)
    gate_out = jnp.zeros(logits.shape, F32)
    rank_out = jnp.zeros(logits.shape, F32)
    for k in range(TOP_K):
        rank = jnp.sum(jnp.where(lane == idxs[k], before, 0.0), axis=-1, keepdims=True)
        idx_out = jnp.where(lane == float(k), idxs[k], idx_out)
        gate_out = jnp.where(lane == float(k), es[k] / den, gate_out)
        rank_out = jnp.where(lane == float(k), rank, rank_out)
    idx_ref[...] = idx_out.astype(I32)
    gate_ref[...] = gate_out
    rank_ref[...] = rank_out.astype(I32)
    carry_ref[0:1, :] = carry_ref[0:1, :] + jnp.sum(sel, axis=0, keepdims=True)
    cnt_ref[...] = jnp.broadcast_to(carry_ref[0:1, :], cnt_ref.shape).astype(I32)


def router(xb, w_r, b_r, layer, tm):
    m, d = xb.shape
    tri = (np.arange(tm)[:, None] > np.arange(tm)[None, :]).astype(np.float32)
    tri = jnp.asarray(tri, BF16)
    row = pl.BlockSpec((tm, LANES), lambda i: (i, 0))
    return pl.pallas_call(
        _router_kernel, grid=(m // tm,),
        in_specs=[pl.BlockSpec((tm, d), lambda i: (i, 0)),
                  pl.BlockSpec((None, d, LANES), lambda i: (layer, 0, 0)),
                  pl.BlockSpec((None, 1, LANES), lambda i: (layer, 0, 0)),
                  pl.BlockSpec((tm, tm), lambda i: (0, 0))],
        out_specs=[row, row, row, pl.BlockSpec((8, LANES), lambda i: (0, 0))],
        out_shape=[jax.ShapeDtypeStruct((m, LANES), I32), jax.ShapeDtypeStruct((m, LANES), F32),
                   jax.ShapeDtypeStruct((m, LANES), I32), jax.ShapeDtypeStruct((8, LANES), I32)],
        scratch_shapes=[pltpu.VMEM((8, LANES), F32)],
        compiler_params=_params("arbitrary"), name="router",
    )(xb, w_r, b_r, tri)


def _dispatch_kernel(info_ref, dest_ref, x_ref, o_ref, zero_ref, sem, fill_sem, *, n_exp, blk, n_blocks, tc):
    @pl.when(pl.program_id(0) == 0)
    def _():
        zero_ref[...] = jnp.zeros_like(zero_ref)

        def per_expert(e, c):
            first = info_ref[e]
            count = info_ref[n_exp + e]

            def start(r, c2):
                pltpu.make_async_copy(zero_ref.at[pl.ds(0, 1), :], o_ref.at[pl.ds(first + r, 1), :], fill_sem).start()
                return c2

            def wait(r, c2):
                pltpu.make_async_copy(zero_ref.at[pl.ds(0, 1), :], o_ref.at[pl.ds(first, 1), :], fill_sem).wait()
                return c2

            lax.fori_loop(0, count, start, 0)
            lax.fori_loop(0, count, wait, 0)
            return c

        lax.fori_loop(0, n_exp, per_expert, 0)
        used = info_ref[2 * n_exp]

        def tail_start(b, c):
            pltpu.make_async_copy(zero_ref, o_ref.at[pl.ds(b * blk, blk), :], fill_sem).start()
            return c

        def tail_wait(b, c):
            pltpu.make_async_copy(zero_ref, o_ref.at[pl.ds(b * blk, blk), :], fill_sem).wait()
            return c

        lax.fori_loop(used, n_blocks, tail_start, 0)
        lax.fori_loop(used, n_blocks, tail_wait, 0)

    def scatter(r, c):
        for k in range(TOP_K):
            pltpu.make_async_copy(x_ref.at[pl.ds(r, 1), :],
                                  o_ref.at[pl.ds(dest_ref[0, 0, k * tc + r], 1), :], sem).start(priority=k % 2)
        return c

    lax.fori_loop(0, tc, scatter, 0, unroll=DMA_UNROLL)
    for k in range(TOP_K):
        pltpu.make_async_copy(x_ref, o_ref.at[pl.ds(0, tc), :], sem).wait()


def moe_dispatch(xp, dest, fill_info, n_exp, blk, n_blocks):
    t, w = xp.shape
    tc = _pick(t, DISPATCH_TC, 8)
    assert blk % 8 == 0
    nsteps = t // tc
    dest_blocks = dest.reshape(TOP_K, nsteps, tc).transpose(1, 0, 2).reshape(nsteps, 1, TOP_K * tc)
    kern = functools.partial(_dispatch_kernel, n_exp=n_exp, blk=blk, n_blocks=n_blocks, tc=tc)
    return pl.pallas_call(
        kern,
        grid_spec=pltpu.PrefetchScalarGridSpec(
            num_scalar_prefetch=1, grid=(nsteps,),
            in_specs=[pl.BlockSpec((1, 1, TOP_K * tc), lambda i, info: (i, 0, 0), memory_space=pltpu.SMEM),
                      pl.BlockSpec((tc, w), lambda i, info: (i, 0))],
            out_specs=pl.BlockSpec(memory_space=pl.ANY),
            scratch_shapes=[pltpu.VMEM((blk, w), xp.dtype), pltpu.SemaphoreType.DMA(()),
                            pltpu.SemaphoreType.DMA(())]),
        out_shape=jax.ShapeDtypeStruct((n_blocks * blk, w), xp.dtype),
        compiler_params=_params("arbitrary"), name="moe_dispatch",
    )(fill_info, dest_blocks, xp)


def _expert_up_kernel(be_ref, valid_ref, x_ref, w_ref, b_ref, o_ref):
    de = o_ref.shape[1]
    half = w_ref.shape[0] // 2

    @pl.when(valid_ref[pl.program_id(0)] != 0)
    def _():
        lo, hi = _unpack_bf16_pairs(x_ref[...])
        gu = (jnp.dot(lo, w_ref[0:half, :], preferred_element_type=F32)
              + jnp.dot(hi, w_ref[half:, :], preferred_element_type=F32) + b_ref[...])
        g = jnp.minimum(gu[:, :de], SWIGLU_LIMIT)
        up = jnp.clip(gu[:, de:], -SWIGLU_LIMIT, SWIGLU_LIMIT)
        o_ref[...] = (g * jax.nn.sigmoid(SWIGLU_ALPHA * g) * (up + 1.0)).astype(o_ref.dtype)

    @pl.when(valid_ref[pl.program_id(0)] == 0)
    def _():
        o_ref[...] = jnp.zeros_like(o_ref)


def _expert_down_kernel(be_ref, valid_ref, a_ref, w_ref, b_ref, o_ref):
    @pl.when(valid_ref[pl.program_id(0)] != 0)
    def _():
        o_ref[...] = _pack_bf16_pairs(jnp.dot(a_ref[...], w_ref[...], preferred_element_type=F32) + b_ref[...])

    @pl.when(valid_ref[pl.program_id(0)] == 0)
    def _():
        o_ref[...] = jnp.zeros_like(o_ref)


def expert_ffn(xs, blk_exp, blk_valid, w_gu, b_gu, w_dn, b_dn, blk):
    p = xs.shape[0]
    d, n2 = w_gu.shape[1], w_gu.shape[2]
    de = n2 // 2
    nblk = p // blk
    act = pl.pallas_call(
        _expert_up_kernel,
        grid_spec=pltpu.PrefetchScalarGridSpec(
            num_scalar_prefetch=2, grid=(nblk,),
            in_specs=[pl.BlockSpec((blk, d // 2), lambda i, be, va: (i, 0)),
                      pl.BlockSpec((None, d, n2), lambda i, be, va: (be[i], 0, 0)),
                      pl.BlockSpec((None, 1, n2), lambda i, be, va: (be[i], 0, 0))],
            out_specs=pl.BlockSpec((blk, de), lambda i, be, va: (i, 0))),
        out_shape=jax.ShapeDtypeStruct((p, de), BF16),
        compiler_params=_params("arbitrary"), name="expert_up",
    )(blk_exp, blk_valid, xs, w_gu, b_gu)
    return pl.pallas_call(
        _expert_down_kernel,
        grid_spec=pltpu.PrefetchScalarGridSpec(
            num_scalar_prefetch=2, grid=(nblk,),
            in_specs=[pl.BlockSpec((blk, de), lambda i, be, va: (i, 0)),
                      pl.BlockSpec((None, de, d), lambda i, be, va: (be[i], 0, 0)),
                      pl.BlockSpec((None, 1, d), lambda i, be, va: (be[i], 0, 0))],
            out_specs=pl.BlockSpec((blk, d // 2), lambda i, be, va: (i, 0))),
        out_shape=jax.ShapeDtypeStruct((p, d // 2), U32),
        compiler_params=_params("arbitrary"), name="expert_down",
    )(blk_exp, blk_valid, act, w_dn, b_dn)


def _combine_kernel(dest_ref, dest_next_ref, y_ref, x_ref, gate_ref, g_ref, b_ref, o_ref, ob_ref, buf, sem, *,
                    alpha, tc):
    i = pl.program_id(0)
    n = TOP_K * tc

    def issue(d_ref, slot):
        def start(r2, c):
            for j in range(2):
                r = 2 * r2 + j
                pltpu.make_async_copy(y_ref.at[pl.ds(d_ref[0, 0, r], 1), :], buf.at[slot, pl.ds(r, 1), :],
                                      sem.at[slot]).start(priority=j)
            return c
        lax.fori_loop(0, n // 2, start, 0, unroll=DMA_UNROLL // 2)

    @pl.when(i == 0)
    def _():
        issue(dest_ref, 0)

    for slot in range(2):
        @pl.when(i % 2 == slot)
        def _(slot=slot):
            @pl.when(i + 1 < pl.num_programs(0))
            def _():
                issue(dest_next_ref, 1 - slot)

            pltpu.make_async_copy(y_ref.at[pl.ds(0, n), :], buf.at[slot], sem.at[slot]).wait()
            gates = gate_ref[...]
            y_lo = y_hi = None
            for k in range(TOP_K):
                p = buf[slot, k * tc:(k + 1) * tc, :]
                lo = gates[:, k:k + 1] * lax.bitcast_convert_type(lax.shift_left(p, jnp.uint32(16)), F32)
                hi = gates[:, k:k + 1] * lax.bitcast_convert_type(p & jnp.uint32(0xFFFF0000), F32)
                y_lo = lo if y_lo is None else y_lo + lo
                y_hi = hi if y_hi is None else y_hi + hi
            y = jnp.concatenate([y_lo, y_hi], axis=1)
            out = _layernorm(alpha * x_ref[...] + y, g_ref[...], b_ref[...])
            o_ref[...] = out
            ob_ref[...] = out.astype(BF16)


def moe_combine_ln(y_rows, dest, gates, x, g, b, layer, alpha):
    t, d = x.shape
    tc = _pick(t, COMBINE_TC, BF16_SUBLANES)
    nblk = t // tc
    dest_blocks = dest.reshape(TOP_K, nblk, tc).transpose(1, 0, 2).reshape(nblk, 1, TOP_K * tc)
    kern = functools.partial(_combine_kernel, alpha=alpha, tc=tc)
    vec = pl.BlockSpec((None, 1, d), lambda i: (layer, 0, 0))
    row = pl.BlockSpec((tc, d), lambda i: (i, 0))
    return pl.pallas_call(
        kern, grid=(nblk,),
        in_specs=[pl.BlockSpec((1, 1, TOP_K * tc), lambda i: (i, 0, 0), memory_space=pltpu.SMEM),
                  pl.BlockSpec((1, 1, TOP_K * tc), lambda i: (jnp.minimum(i + 1, nblk - 1), 0, 0),
                               memory_space=pltpu.SMEM),
                  pl.BlockSpec(memory_space=pl.ANY),
                  row, pl.BlockSpec((tc, LANES), lambda i: (i, 0)), vec, vec],
        out_specs=[row, row],
        out_shape=[jax.ShapeDtypeStruct((t, d), F32), jax.ShapeDtypeStruct((t, d), BF16)],
        scratch_shapes=[pltpu.VMEM((2, TOP_K * tc, d // 2), U32), pltpu.SemaphoreType.DMA((2,))],
        compiler_params=_params("arbitrary"), name="moe_combine_ln",
    )(dest_blocks, dest_blocks, y_rows, x, gates, g, b)


def _route_plan(idx, rank, counts, n_exp, blk, n_blocks):
    padded = (counts + blk - 1) // blk * blk
    pad_end = jnp.cumsum(padded)
    pad_start = pad_end - padded
    experts = jnp.arange(n_exp, dtype=I32)
    idx_t, rank_t = idx.T, rank.T
    dest = rank_t + jnp.sum(jnp.where(idx_t[None] == experts[:, None, None], pad_start[:, None, None], 0), axis=0)
    block_row0 = jnp.arange(n_blocks, dtype=I32) * blk
    blk_exp = jnp.minimum(jnp.sum((pad_end[None, :] <= block_row0[:, None]).astype(I32), axis=1), n_exp - 1)
    blk_valid = (block_row0 < pad_end[-1]).astype(I32)
    fill_info = jnp.concatenate([pad_start + counts, padded - counts, pad_end[-1:] // blk]).astype(I32)
    return dest.astype(I32), blk_exp.astype(I32), blk_valid, fill_info


def _rope_tables(pos, d_rope):
    half = d_rope // 2
    inv = 1.0 / (ROPE_THETA ** (jnp.arange(half, dtype=F32) * 2.0 / d_rope))
    ang = pos.astype(F32)[:, None] * inv[None, :]
    pad = jnp.zeros((pos.shape[0], LANES - d_rope), F32)
    cos = jnp.concatenate([jnp.cos(ang), jnp.cos(ang), pad], axis=1)
    sin = jnp.concatenate([jnp.sin(ang), jnp.sin(ang), pad], axis=1)
    return cos, sin


def _rel_bucket(rel):
    nb = N_BUCKETS // 2
    max_exact = nb // 2
    ret = jnp.where(rel > 0, nb, 0)
    n = jnp.abs(rel)
    nf = jnp.maximum(n, 1).astype(F32)
    large = max_exact + (jnp.log(nf / max_exact) / math.log(REL_MAX_DIST / max_exact) * (nb - max_exact)).astype(I32)
    large = jnp.minimum(large, nb - 1)
    return (ret + jnp.where(n < max_exact, n, large)).astype(I32)


def _bias_table(rel_bias, q_pos, k_pos):
    bucket = _rel_bucket(k_pos[None, :] - q_pos[:, None])
    vis = (k_pos[None, :] // CHUNK) <= (q_pos[:, None] // CHUNK)
    buckets = jnp.arange(N_BUCKETS, dtype=I32)
    bias = jnp.sum(jnp.where(bucket[None, None] == buckets[:, None, None, None], rel_bias[:, :, None, None], 0.0),
                   axis=0)
    return jnp.where(vis[None], bias, NEG_INF).astype(F32)


def _rot_half_cols(w):
    half = w.shape[-1] // 2
    return jnp.concatenate([-w[..., half:], w[..., :half]], axis=-1)


def _pad_cols(w, n):
    return jnp.pad(w, [(0, 0)] * (w.ndim - 1) + [(0, n - w.shape[-1])])


def kernel(x_prompt, x_sample, cache_conv, cache_mla_ckv, cache_mla_kpe, cache_diff_k, cache_diff_v, w_in, w_dw, b_dw, cn_g, cn_b, w_a_out, qn_g, w_uq, kvn_g, w_ukv, w_b_out, lam_q1, lam_k1, lam_q2, lam_k2, sub_g, w_c_out, rel_bias, w_out, ln1_g, ln1_b, w_router, b_router, w_gu, b_gu, w_dn, b_dn, ln2_g, ln2_b):
    nbp, sp, d = x_prompt.shape
    nbd, sd, _ = x_sample.shape
    depth = w_in.shape[0]
    past = cache_mla_ckv.shape[2]
    skd = past + sd
    taps, c_conv = w_dw.shape[1], w_dw.shape[2]
    q_lora, kv_lora = qn_g.shape[1], kvn_g.shape[1]
    h_b, d_qk = w_uq.shape[2], w_uq.shape[3]
    d_rope = cache_mla_kpe.shape[3]
    d_nope = d_qk - d_rope
    d_vh = w_ukv.shape[3] - d_nope
    h_c, d_hc = cache_diff_k.shape[3], cache_diff_k.shape[4]
    n_exp, d_exp = w_gu.shape[1], w_dn.shape[2]
    n_c = h_c * d_hc
    dn_alpha = (2 * depth) ** 0.25
    mla_scale = d_qk ** -0.5
    diff_scale = (d_hc // 2) ** -0.5
    assert d_nope == LANES and d_vh == LANES and d_hc == LANES and d_rope <= LANES
    assert past % CHUNK == 0 and sd <= CHUNK and n_exp <= LANES

    tp, td = nbp * sp, nbd * sd
    t_all = tp + td
    tm = _pick(t_all, MM_TM, BF16_SUBLANES)
    tm_p = _pick(tp, MM_TM, BF16_SUBLANES)
    tm_kd = _pick(nbd * skd, MM_TM, BF16_SUBLANES)
    halo = -(-(taps - 1) // 8) * 8
    n_assign = t_all * TOP_K
    n_blocks = (n_assign + n_exp * (MOE_BLK - 1) + MOE_BLK - 1) // MOE_BLK

    pos_p = jnp.arange(sp, dtype=I32)
    pos_d = past + jnp.arange(sd, dtype=I32)
    pos_all = jnp.concatenate([jnp.tile(pos_p, nbp), jnp.tile(pos_d, nbd)])
    cos_all, sin_all = _rope_tables(pos_all, d_rope)
    tq_p = _att_tiles(sp, sp)[0]
    if sp > tq_p:
        bias_p = _bias_table(rel_bias, pos_p, jnp.arange(tq_p, dtype=I32)).reshape(h_c, sp // tq_p, tq_p, tq_p)
    else:
        bias_p = _bias_table(rel_bias, pos_p, pos_p)[:, None]
    bias_d = _bias_table(rel_bias, pos_d, jnp.arange(skd, dtype=I32))[:, None]

    sizes = [c_conv, c_conv, q_lora, kv_lora, d_rope, n_c, n_c, n_c, 3 * d]
    offs = [0] + [int(v) for v in np.cumsum(sizes)]
    w_in_t = jnp.swapaxes(w_in, 1, 2).astype(BF16)
    w_kpe_t = w_in_t[:, offs[4]:offs[5], :]
    w_kpe_rot_t = jnp.pad(jnp.concatenate([-w_kpe_t[:, d_rope // 2:], w_kpe_t[:, :d_rope // 2]], axis=1),
                          ((0, 0), (0, LANES - d_rope), (0, 0)))
    w_q1 = _pad_cols(w_uq, 2 * LANES).reshape(depth, q_lora, h_b * 2 * LANES).astype(BF16)
    w_q2 = _pad_cols(_rot_half_cols(w_uq[..., d_nope:]), LANES).reshape(depth, q_lora, h_b * LANES).astype(BF16)
    w_kv = w_ukv.reshape(depth, kv_lora, h_b * (d_nope + d_vh)).astype(BF16)
    w_a, w_b, w_c, w_o = (w.astype(BF16) for w in (w_a_out, w_b_out, w_c_out, w_out))
    w_r = _pad_cols(w_router, LANES).astype(BF16)
    b_r = jnp.concatenate([b_router, jnp.full((depth, LANES - n_exp), NEG_INF, F32)], axis=1).reshape(depth, 1, LANES)
    b_gu3, b_dn3 = b_gu.reshape(depth, n_exp, 1, 2 * d_exp), b_dn.reshape(depth, n_exp, 1, d)
    w_gu2 = w_gu.reshape(depth * n_exp * d, 2 * d_exp)
    w_dn2 = w_dn.reshape(depth * n_exp * d_exp, d)
    vec3 = lambda a: a.reshape(depth, 1, a.shape[-1])
    qn_g3, kvn_g3, b_dw3, cn_g3, cn_b3, sub_g3 = map(vec3, (qn_g, kvn_g, b_dw, cn_g, cn_b, sub_g))
    ln1_g3, ln1_b3, ln2_g3, ln2_b3 = map(vec3, (ln1_g, ln1_b, ln2_g, ln2_b))
    lam3 = tuple(map(vec3, (lam_q1, lam_k1, lam_q2, lam_k2)))
    hist_p = jnp.zeros((nbp, halo, c_conv), F32)
    hist_d_all = jnp.pad(cache_conv, ((0, 0), (0, 0), (halo - (taps - 1), 0), (0, 0)))
    ckv_cache_b = cache_mla_ckv.astype(BF16)
    kpe_cache_b = _pad_cols(cache_mla_kpe, LANES).astype(BF16)
    dk_cache_b = cache_diff_k.astype(BF16).reshape(depth, nbd, past, n_c)
    dv_cache_b = cache_diff_v.astype(BF16).reshape(depth, nbd, past, n_c)

    x = jnp.concatenate([x_prompt.reshape(tp, d), x_sample.reshape(td, d)], axis=0)
    xb = x.astype(BF16)
    tile_ij = lambda i, j: (i, j)
    tile_i0 = lambda i, j: (i, 0)
    states = [[] for _ in range(6)]
    dk_all = jnp.zeros((depth, t_all, n_c), F32)
    dv_all = jnp.zeros((depth, t_all, n_c), F32)

    for layer in range(depth):
        lam_init = 0.8 - 0.6 * math.exp(-0.3 * layer)
        lvec = lambda n: ((None, 1, n), lambda i, j, layer=layer: (layer, 0, 0))
        mm = functools.partial(fused_matmul, layer=layer)

        tn = _pick(c_conv, MM_TN, LANES)
        glu, = mm([xb], [(0, w_in_t, tn, offs[0], True), (0, w_in_t, tn, offs[1], True)], [],
                  lambda a, e: [a[0] * jax.nn.sigmoid(a[1])], [(c_conv, tn, F32)],
                  m_rows=t_all, n_tiles=c_conv // tn, tm=tm, name="in_glu")
        hq, = mm([xb], [(0, w_in_t, q_lora, offs[2], True)], [(qn_g3,) + lvec(q_lora)],
                 lambda a, e: [_rms(a[0], e[0])], [(q_lora, q_lora, BF16)],
                 m_rows=t_all, n_tiles=1, tm=tm, name="in_cq")
        ckv, ckv_b = mm([xb], [(0, w_in_t, kv_lora, offs[3], True)], [(kvn_g3,) + lvec(kv_lora)],
                        lambda a, e: [_rms(a[0], e[0])] * 2,
                        [(kv_lora, kv_lora, F32), (kv_lora, kv_lora, BF16)],
                        m_rows=t_all, n_tiles=1, tm=tm, name="in_ckv")
        kpe, kpe_b = mm([xb], [(0, w_in_t, LANES, offs[4], True), (0, w_kpe_rot_t, LANES, 0, True)],
                        [(cos_all, (tm, LANES), tile_i0), (sin_all, (tm, LANES), tile_i0)],
                        lambda a, e: [a[0] * e[0] + a[1] * e[1]] * 2, [(LANES, LANES, F32), (LANES, LANES, BF16)],
                        m_rows=t_all, n_tiles=1, tm=tm, name="in_kpe")
        tn = _pick(n_c, MM_TN, LANES)
        nq_t = n_c // tn

        def qkv_epilogue(a, e, nq_t=nq_t):
            j = pl.program_id(1)
            return [jnp.where(j < nq_t, a[0] * diff_scale, a[0])]

        def kv_state(first_tile, buf, nq_t=nq_t, tn=tn):
            return ((depth, t_all, n_c), F32, (None, tm, tn),
                    lambda i, j: (layer, i, jnp.clip(j - first_tile, 0, nq_t - 1)),
                    lambda i, j: (j >= first_tile) & (j < first_tile + nq_t), lambda a, e: a[0], buf)

        dqkv_b, dk_all, dv_all = mm([xb], [(0, w_in_t, tn, offs[5], True)], [], qkv_epilogue,
                                    [(3 * n_c, tn, BF16)], m_rows=t_all, n_tiles=3 * nq_t, tm=tm,
                                    placed=[kv_state(nq_t, dk_all), kv_state(2 * nq_t, dv_all)], name="in_dqkv")
        tn = _pick(3 * d, MM_TN, LANES)
        gate_steps = (t_all // tm) * (3 * d // tn)
        riders = []
        for w_all in (w_gu2, w_dn2):
            per_layer = w_all.shape[0] // depth
            rows = _rider_rows(per_layer, gate_steps)
            riders.append((w_all, rows, per_layer // rows, layer * (per_layer // rows)))
        gates, w_gu_b, w_dn_b = mm([xb], [(0, w_in_t, tn, offs[8], True)], [], lambda a, e: [jax.nn.sigmoid(a[0])],
                                   [(3 * d, tn, BF16)], m_rows=t_all, n_tiles=3 * d // tn, tm=tm, riders=riders,
                                   name="in_gates")

        conv_args = (w_dw, b_dw3, cn_g3, cn_b3, layer)
        ca = conv_module(glu, 0, nbp, sp, hist_p, *conv_args, t_all, into=jnp.zeros((t_all, c_conv), BF16))
        ca = conv_module(glu, tp, nbd, sd, hist_d_all[layer], *conv_args, t_all, into=ca)
        glu_p = glu[:tp].reshape(nbp, sp, c_conv)
        glu_d = glu[tp:].reshape(nbd, sd, c_conv)
        new_conv_p = jnp.concatenate([jnp.zeros((nbp, taps - 1, c_conv), F32), glu_p], axis=1)[:, -(taps - 1):]
        new_conv_d = jnp.concatenate([cache_conv[layer], glu_d], axis=1)[:, -(taps - 1):]

        def q_epilogue(a, e):
            rope = a[0][:, LANES:] * e[0] + a[1] * e[1]
            return [jnp.concatenate([a[0][:, :LANES], rope], axis=1) * mla_scale]

        q_mla, = mm([hq], [(0, w_q1, 2 * LANES, 0), (0, w_q2, LANES, 0)],
                    [(cos_all, (tm, LANES), tile_i0), (sin_all, (tm, LANES), tile_i0)],
                    q_epilogue, [(h_b * 2 * LANES, 2 * LANES, BF16)],
                    m_rows=t_all, n_tiles=h_b, tm=tm, name="mla_q")

        def kv_epilogue(a, e):
            return [jnp.concatenate([a[0][:, :LANES], e[0].astype(F32)], axis=1), a[0][:, LANES:]]

        kv_outs = [(h_b * 2 * LANES, 2 * LANES, BF16), (h_b * d_vh, d_vh, BF16)]
        k_p, v_p = mm([ckv_b], [(0, w_kv, 2 * LANES, 0)], [(kpe_b, (tm_p, LANES), tile_i0)],
                      kv_epilogue, kv_outs, m_rows=tp, n_tiles=h_b, tm=tm_p, name="mla_kv_prompt")
        ckv_hist = jnp.concatenate([ckv_cache_b[layer], ckv_b[tp:].reshape(nbd, sd, kv_lora)],
                                   axis=1).reshape(nbd * skd, kv_lora)
        kpe_hist = jnp.concatenate([kpe_cache_b[layer], kpe_b[tp:].reshape(nbd, sd, LANES)],
                                   axis=1).reshape(nbd * skd, LANES)
        k_d, v_d = mm([ckv_hist], [(0, w_kv, 2 * LANES, 0)], [(kpe_hist, (tm_kd, LANES), tile_i0)],
                      kv_epilogue, kv_outs, m_rows=nbd * skd, n_tiles=h_b, tm=tm_kd, name="mla_kv_decode")
        ob = mla_attention(q_mla, k_p, v_p, nb=nbp, nh=h_b, sq=sp, sk=sp, q_off=0, q_row_off=0, total_rows=t_all,
                           into=jnp.zeros((t_all, h_b * d_vh), BF16))
        ob = mla_attention(q_mla, k_d, v_d, nb=nbd, nh=h_b, sq=sd, sk=skd, q_off=past, q_row_off=tp,
                           total_rows=t_all, into=ob)

        dkv_new = dqkv_b[tp:, n_c:].reshape(nbd, sd, 2 * n_c)
        dk_hist = jnp.concatenate([dk_cache_b[layer], dkv_new[:, :, :n_c]], axis=1).reshape(nbd * skd, n_c)
        dv_hist = jnp.concatenate([dv_cache_b[layer], dkv_new[:, :, n_c:]], axis=1).reshape(nbd * skd, n_c)
        oc = diff_attention(dqkv_b, dqkv_b, dqkv_b, bias_p, lam3, sub_g3, lam_init, layer,
                            nb=nbp, nh=h_c, sq=sp, sk=sp, q_off=0, q_row_off=0, q_col=0, k_col=h_c, v_col=2 * h_c,
                            total_rows=t_all, into=jnp.zeros((t_all, n_c), BF16))
        oc = diff_attention(dqkv_b, dk_hist, dv_hist, bias_d, lam3, sub_g3, lam_init, layer,
                            nb=nbd, nh=h_c, sq=sd, sk=skd, q_off=past, q_row_off=tp, q_col=0, k_col=0, v_col=0,
                            total_rows=t_all, into=oc)

        tn = _pick(d, MM_TN, LANES)
        nt = d // tn
        merged, = mm(
            [ca, ob, oc], [(0, w_a, tn, 0), (1, w_b, tn, 0), (2, w_c, tn, 0)],
            [(gates, (tm, tn), tile_ij), (gates, (tm, tn), lambda i, j, nt=nt: (i, j + nt)),
             (gates, (tm, tn), lambda i, j, nt=nt: (i, j + 2 * nt))],
            lambda a, e: [e[0].astype(F32) * a[0] + e[1].astype(F32) * a[1] + e[2].astype(F32) * a[2]],
            [(d, tn, BF16)], m_rows=t_all, n_tiles=nt, tm=tm, name="merge")
        t1, = mm([merged], [(0, w_o, tn, 0)], [(x, (tm, tn), tile_ij)],
                 lambda a, e: [dn_alpha * e[0] + a[0]], [(d, tn, F32)],
                 m_rows=t_all, n_tiles=nt, tm=tm, name="w_out")
        x1, x1b, x1p = layernorm_rows(t1, ln1_g3, ln1_b3, layer)

        top_i, top_g, top_rank, counts = router(x1b, w_r, b_r, layer, tm)
        dest, blk_exp, blk_valid, fill_info = _route_plan(
            top_i[:, :TOP_K], top_rank[:, :TOP_K], counts[0, :n_exp], n_exp, MOE_BLK, n_blocks)
        xs = moe_dispatch(x1p, dest, fill_info, n_exp, MOE_BLK, n_blocks)
        y_rows = expert_ffn(xs, blk_exp, blk_valid, w_gu_b.reshape(n_exp, d, 2 * d_exp), b_gu3[layer],
                            w_dn_b.reshape(n_exp, d_exp, d), b_dn3[layer], MOE_BLK)
        x, xb = moe_combine_ln(y_rows, dest, top_g, x1, ln2_g3, ln2_b3, layer, dn_alpha)

        outs = (new_conv_p, ckv[:tp].reshape(nbp, sp, kv_lora), kpe[:tp, :d_rope].reshape(nbp, sp, d_rope),
                new_conv_d, ckv[tp:].reshape(nbd, sd, kv_lora), kpe[tp:, :d_rope].reshape(nbd, sd, d_rope))
        for buf, s in zip(states, outs):
            buf.append(s)

    y_prompt = x[:tp].reshape(nbp, sp, d)
    y_sample = x[tp:].reshape(nbd, sd, d)
    p_conv, p_ckv, p_kpe, s_conv, s_ckv, s_kpe = (jnp.stack(b) for b in states)
    return (y_prompt, y_sample, p_conv, p_ckv, p_kpe,
            dk_all[:, :tp].reshape(depth, nbp, sp, h_c, d_hc), dv_all[:, :tp].reshape(depth, nbp, sp, h_c, d_hc),
            s_conv, s_ckv, s_kpe,
            dk_all[:, tp:].reshape(depth, nbd, sd, h_c, d_hc), dv_all[:, tp:].reshape(depth, nbd, sd, h_c, d_hc))
```
